```python
import jax, jax.numpy as jnp
from jax import lax
import numpy as np

D_MODEL = 2048
BATCH = 2
SEQ = 8192
DEPTH = 4

N_MIXERS = 3
N_GLA = (DEPTH + 2) // 3
N_POOL = (DEPTH + 1) // 3
N_DSA = DEPTH // 3

GLA_HEADS = 4
GLA_DK = D_MODEL // 2 // GLA_HEADS
GLA_DV = D_MODEL // GLA_HEADS
GLA_GATE_RANK = 16
GLA_TAU = 16.0
GLA_CHUNK = 64
GLA_IN_SIZES = (GLA_HEADS * GLA_DK, GLA_HEADS * GLA_DK, GLA_HEADS * GLA_DV, GLA_HEADS * GLA_DV)

POOL_WINDOWS = (2, 4, 8, 16)
POOL_GROUPS = len(POOL_WINDOWS)
POOL_GW = D_MODEL // POOL_GROUPS

DSA_HEADS = 16
DSA_KV_HEADS = 4
DSA_HEAD_DIM = D_MODEL // DSA_HEADS
DSA_IDX_HEADS = 16
DSA_IDX_DIM = 64
DSA_TOPK_MAX = 256
DSA_QBLOCK = 128
DSA_IN_SIZES = (DSA_HEADS * DSA_HEAD_DIM, DSA_KV_HEADS * DSA_HEAD_DIM, DSA_KV_HEADS * DSA_HEAD_DIM,
                DSA_IDX_HEADS * DSA_IDX_DIM, DSA_IDX_DIM, DSA_IDX_HEADS)

D_FF = ((8 * D_MODEL // 3 + 127) // 128) * 128
CONV_WIDTH = 3

PLE_DIM = 256
LN_EPS = 1e-5
ALPHA = (2.0 * DEPTH) ** 0.25
BETA = (8.0 * DEPTH) ** -0.25

kernel_name = "hybrid_gla_pool_dsa_convffn_deepnorm"


def _split_points(sizes):
    pts, acc = [], 0
    for s in sizes[:-1]:
        acc += s
        pts.append(acc)
    return pts


def _layer_norm(x, g, b):
    xf = x.astype(jnp.float32)
    mu = jnp.mean(xf, -1, keepdims=True)
    var = jnp.mean(jnp.square(xf - mu), -1, keepdims=True)
    return ((xf - mu) * lax.rsqrt(var + LN_EPS) * g.astype(jnp.float32) + b.astype(jnp.float32)).astype(x.dtype)


def _gla_mixer(x, w_in, w_a1, w_a2, b_a, norm_g, w_o):
    B, L, _ = x.shape
    H, DK, DV, C = GLA_HEADS, GLA_DK, GLA_DV, GLA_CHUNK
    n = L // C
    q, k, v, r = jnp.split(x @ w_in, _split_points(GLA_IN_SIZES), axis=-1)
    log_a = jax.nn.log_sigmoid(((x @ w_a1) @ w_a2 + b_a).astype(jnp.float32)) / GLA_TAU

    def to_chunks(t, d):
        return t.reshape(B, n, C, H, d).transpose(1, 0, 3, 2, 4).astype(jnp.float32)

    qc = to_chunks(q, DK) * (DK ** -0.5)
    kc = to_chunks(k, DK)
    vc = to_chunks(v, DV)
    bc = jnp.cumsum(to_chunks(log_a, DK), axis=3)
    causal = jnp.tril(jnp.ones((C, C), dtype=bool))[:, :, None]

    def step(S, inp):
        qt, kt, vt, bt = inp
        o_inter = jnp.einsum('bhtk,bhkv->bhtv', qt * jnp.exp(bt), S)
        diff = bt[:, :, :, None, :] - bt[:, :, None, :, :]
        decay = jnp.exp(jnp.where(causal, diff, -jnp.inf))
        scores = jnp.einsum('bhtk,bhsk,bhtsk->bhts', qt, kt, decay)
        o_intra = jnp.einsum('bhts,bhsv->bhtv', scores, vt)
        b_last = bt[:, :, -1:, :]
        S_new = S * jnp.exp(b_last[:, :, 0, :, None]) + jnp.einsum('bhsk,bhsv->bhkv', kt * jnp.exp(b_last - bt), vt)
        return S_new, o_inter + o_intra

    S0 = jnp.zeros((B, H, DK, DV), jnp.float32)
    _, o = lax.scan(step, S0, (qc, kc, vc, bc))
    o = o * lax.rsqrt(jnp.mean(o * o, -1, keepdims=True) + LN_EPS)
    o = o.transpose(1, 0, 3, 2, 4).reshape(B, L, H, DV) * norm_g.astype(jnp.float32).reshape(H, DV)
    o = o.reshape(B, L, H * DV).astype(x.dtype) * jax.nn.silu(r)
    return o @ w_o


def _pool_mixer(x, w_grp, scale):
    B, L, D = x.shape
    xf = x.astype(jnp.float32)
    cs = jnp.cumsum(xf, axis=1)
    t = jnp.arange(L)
    outs = []
    for g, w in enumerate(POOL_WINDOWS):
        csg = cs[:, :, g * POOL_GW:(g + 1) * POOL_GW]
        shifted = jnp.pad(csg, ((0, 0), (w, 0), (0, 0)))[:, :L]
        cnt = jnp.minimum(t + 1, w).astype(jnp.float32)[None, :, None]
        outs.append((csg - shifted) / cnt - xf[:, :, g * POOL_GW:(g + 1) * POOL_GW])
    pooled = jnp.stack(outs, axis=2).astype(x.dtype)
    y = jnp.einsum('blgc,gcd->blgd', pooled, w_grp).reshape(B, L, D)
    return y * scale


def _dsa_mixer(x, w_in, kidx_g, kidx_b, w_o):
    B, L, _ = x.shape
    G, R, Dh = DSA_KV_HEADS, DSA_HEADS // DSA_KV_HEADS, DSA_HEAD_DIM
    HI, DI, QB = DSA_IDX_HEADS, DSA_IDX_DIM, DSA_QBLOCK
    topk = min(DSA_TOPK_MAX, L // 4)
    nb = L // QB
    q, k, v, qi, ki, wi = jnp.split(x @ w_in, _split_points(DSA_IN_SIZES), axis=-1)
    q = q.reshape(B, L, G, R, Dh)
    k = k.reshape(B, L, G, Dh)
    v = v.reshape(B, L, G, Dh)
    qi = qi.reshape(B, L, HI, DI).astype(jnp.float32)
    ki = _layer_norm(ki, kidx_g, kidx_b).astype(jnp.float32)
    wi = wi.astype(jnp.float32) * (HI ** -0.5) * (DI ** -0.5)

    def blockify(t):
        return t.reshape((B, nb, QB) + t.shape[2:]).swapaxes(0, 1)

    s_pos = jnp.arange(L)

    def block(args):
        blk, qb, qib, wib = args
        t_pos = blk * QB + jnp.arange(QB)
        rel = jax.nn.relu(jnp.einsum('bthd,bsd->bths', qib, ki))
        idx_score = jnp.einsum('bth,bths->bts', wib, rel)
        admissible = s_pos[None, :] <= t_pos[:, None]
        idx_score = jnp.where(admissible[None], idx_score, -jnp.inf)
        _, sel = lax.top_k(idx_score, topk)
        valid = sel <= t_pos[None, :, None]
        ksel = jax.vmap(lambda kb, ib: kb[ib])(k, sel).astype(jnp.float32)
        vsel = jax.vmap(lambda vb, ib: vb[ib])(v, sel).astype(jnp.float32)
        s = jnp.einsum('btgrd,btkgd->btgrk', qb.astype(jnp.float32), ksel) * (Dh ** -0.5)
        s = jnp.where(valid[:, :, None, None, :], s, -jnp.inf)
        prob = jax.nn.softmax(s, axis=-1)
        o = jnp.einsum('btgrk,btkgd->btgrd', prob, vsel)
        return o.reshape(B, QB, G * R * Dh).astype(x.dtype)

    o = lax.map(block, (jnp.arange(nb), blockify(q), blockify(qi), blockify(wi)))
    o = o.swapaxes(0, 1).reshape(B, L, G * R * Dh)
    return o @ w_o


def _conv_ffn(x, w_up, conv_w, conv_b, w_down):
    L = x.shape[1]
    h = x @ w_up
    hp = jnp.pad(h, ((0, 0), (CONV_WIDTH - 1, 0), (0, 0)))
    hc = conv_b + hp[:, 0:L] * conv_w[0]
    for j in range(1, CONV_WIDTH):
        hc = hc + hp[:, j:j + L] * conv_w[j]
    g, u = jnp.split(hc, 2, axis=-1)
    return (jax.nn.gelu(g, approximate=False) * u) @ w_down


def setup_inputs(seed: int = 0) -> dict:
    key = jax.random.key(seed)
    ks = jax.random.split(key, 24)
    f32 = jnp.float32

    def nrm(k, shape, scale):
        return jax.random.normal(k, shape, f32) * scale

    D = D_MODEL
    return {
        "x": nrm(ks[0], (BATCH, SEQ, D), 1.0),
        "p": nrm(ks[1], (DEPTH, BATCH, SEQ, PLE_DIM), 1.0),
        "gla_w_in": nrm(ks[2], (N_GLA, D, sum(GLA_IN_SIZES)), D ** -0.5),
        "gla_w_a1": nrm(ks[3], (N_GLA, D, GLA_GATE_RANK), D ** -0.5),
        "gla_w_a2": nrm(ks[4], (N_GLA, GLA_GATE_RANK, GLA_HEADS * GLA_DK), GLA_GATE_RANK ** -0.5),
        "gla_b_a": nrm(ks[5], (N_GLA, GLA_HEADS * GLA_DK), 0.1),
        "gla_norm_g": 1.0 + nrm(ks[6], (N_GLA, GLA_HEADS * GLA_DV), 0.02),
        "gla_w_o": nrm(ks[7], (N_GLA, GLA_HEADS * GLA_DV, D), (GLA_HEADS * GLA_DV) ** -0.5 * BETA),
        "pool_w": nrm(ks[8], (N_POOL, POOL_GROUPS, POOL_GW, POOL_GW), POOL_GW ** -0.5 * BETA),
        "pool_scale": 1.0 + nrm(ks[9], (N_POOL, D), 0.02),
        "dsa_w_in": nrm(ks[10], (N_DSA, D, sum(DSA_IN_SIZES)), D ** -0.5),
        "dsa_kidx_g": 1.0 + nrm(ks[11], (N_DSA, DSA_IDX_DIM), 0.02),
        "dsa_kidx_b": nrm(ks[12], (N_DSA, DSA_IDX_DIM), 0.02),
        "dsa_w_o": nrm(ks[13], (N_DSA, DSA_HEADS * DSA_HEAD_DIM, D), (DSA_HEADS * DSA_HEAD_DIM) ** -0.5 * BETA),
        "ln_mix_g": 1.0 + nrm(ks[14], (DEPTH, D), 0.02),
        "ln_mix_b": nrm(ks[15], (DEPTH, D), 0.02),
        "ffn_w_up": nrm(ks[16], (DEPTH, D, 2 * D_FF), D ** -0.5),
        "ffn_conv_w": nrm(ks[17], (DEPTH, CONV_WIDTH, 2 * D_FF), CONV_WIDTH ** -0.5),
        "ffn_conv_b": nrm(ks[18], (DEPTH, 2 * D_FF), 0.02),
        "ffn_w_down": nrm(ks[19], (DEPTH, D_FF, D), D_FF ** -0.5 * BETA),
        "ple_gate_w": nrm(ks[20], (DEPTH, D, D), D ** -0.5),
        "ple_proj_w": nrm(ks[21], (DEPTH, PLE_DIM, D), PLE_DIM ** -0.5 * BETA),
        "ln_ffn_g": 1.0 + nrm(ks[22], (DEPTH, D), 0.02),
        "ln_ffn_b": nrm(ks[23], (DEPTH, D), 0.02),
    }


def reference(x, p, gla_w_in, gla_w_a1, gla_w_a2, gla_b_a, gla_norm_g, gla_w_o,
              pool_w, pool_scale, dsa_w_in, dsa_kidx_g, dsa_kidx_b, dsa_w_o,
              ln_mix_g, ln_mix_b, ffn_w_up, ffn_conv_w, ffn_conv_b, ffn_w_down,
              ple_gate_w, ple_proj_w, ln_ffn_g, ln_ffn_b):
    for i in range(DEPTH):
        kind, j = i % N_MIXERS, i // N_MIXERS
        if kind == 0:
            m = _gla_mixer(x, gla_w_in[j], gla_w_a1[j], gla_w_a2[j], gla_b_a[j], gla_norm_g[j], gla_w_o[j])
        elif kind == 1:
            m = _pool_mixer(x, pool_w[j], pool_scale[j])
        else:
            m = _dsa_mixer(x, dsa_w_in[j], dsa_kidx_g[j], dsa_kidx_b[j], dsa_w_o[j])
        x = _layer_norm(ALPHA * x + m, ln_mix_g[i], ln_mix_b[i])
        ple = jax.nn.sigmoid(x @ ple_gate_w[i]) * (p[i] @ ple_proj_w[i])
        f = _conv_ffn(x, ffn_w_up[i], ffn_conv_w[i], ffn_conv_b[i], ffn_w_down[i])
        x = _layer_norm(ALPHA * x + f + ple, ln_ffn_g[i], ln_ffn_b[i])
    return x
```

```python
import functools

import jax
import jax.numpy as jnp
from jax import lax
from jax.experimental import pallas as pl
from jax.experimental.pallas import tpu as pltpu

F32 = jnp.float32
BF16 = jnp.bfloat16

DEPTH = 4
N_MIXERS = 3
LN_EPS = 1e-5
ALPHA = (2.0 * DEPTH) ** 0.25
GLA_HEADS = 4
GLA_TAU = 16.0
POOL_WINDOWS = (2, 4, 8, 16)
DSA_HEADS = 16
DSA_KV_HEADS = 4
DSA_IDX_HEADS = 16
DSA_IDX_DIM = 64
DSA_TOPK_MAX = 256
CONV_WIDTH = 3

LANES = 128
BF16_SUBLANES = 16
VMEM_LIMIT_BYTES = 56 * 1024 * 1024

NEG_BIG = -1e30
INT_MIN = -(2 ** 31)
KEY_NEG_FLT_MAX = INT_MIN + 0x800000


def _params(*sem):
    return pltpu.CompilerParams(dimension_semantics=sem, vmem_limit_bytes=VMEM_LIMIT_BYTES)


def _dot(a, b):
    return jnp.dot(a, b, preferred_element_type=F32)


def _dot_nt(a, b):
    return lax.dot_general(a, b, (((1,), (1,)), ((), ())), preferred_element_type=F32)


def _layer_norm(y, g, b):
    mu = jnp.mean(y, -1, keepdims=True)
    yc = y - mu
    var = jnp.mean(yc * yc, -1, keepdims=True)
    return yc * lax.rsqrt(var + LN_EPS) * g + b


def _matmul_kernel(x_ref, w_ref, o_ref):
    o_ref[...] = _dot(x_ref[...], w_ref[...]).astype(o_ref.dtype)


def _matmul(x, w, out_dtype, tm, tn):
    m, k = x.shape
    n = w.shape[1]
    return pl.pallas_call(
        _matmul_kernel,
        grid=(m // tm, n // tn),
        in_specs=[pl.BlockSpec((tm, k), lambda i, j: (i, 0)),
                  pl.BlockSpec((k, tn), lambda i, j: (0, j))],
        out_specs=pl.BlockSpec((tm, tn), lambda i, j: (i, j)),
        out_shape=jax.ShapeDtypeStruct((m, n), out_dtype),
        compiler_params=_params("parallel", "parallel"),
        name="matmul",
    )(x, w)


def _proj_res_ln_kernel(a_ref, w_ref, x_ref, g_ref, b_ref, of_ref, ob_ref):
    m = _dot(a_ref[...], w_ref[...])
    y = _layer_norm(ALPHA * x_ref[...] + m, g_ref[...], b_ref[...])
    of_ref[...] = y
    ob_ref[...] = y.astype(BF16)


def _proj_res_ln(a, w, x, g, b, tm=256):
    m, ka = a.shape
    d = w.shape[1]
    row = lambda i: (i, 0)
    const = lambda i: (0, 0)
    return pl.pallas_call(
        _proj_res_ln_kernel,
        grid=(m // tm,),
        in_specs=[pl.BlockSpec((tm, ka), row), pl.BlockSpec((ka, d), const),
                  pl.BlockSpec((tm, d), row), pl.BlockSpec((1, d), const),
                  pl.BlockSpec((1, d), const)],
        out_specs=[pl.BlockSpec((tm, d), row), pl.BlockSpec((tm, d), row)],
        out_shape=[jax.ShapeDtypeStruct((m, d), F32), jax.ShapeDtypeStruct((m, d), BF16)],
        compiler_params=_params("parallel"),
        name="proj_res_ln",
    )(a, w, x, g, b)


def _ffn_kernel(x_ref, halo_ref, wg_ref, wu_ref, cg_ref, cu_ref, wd_ref, o_ref,
                xs_ref, hg_ref, hu_ref, *, tm, blocks_per_seq):
    i = pl.program_id(0)
    j = pl.program_id(1)
    hr = BF16_SUBLANES

    @pl.when(j == 0)
    def _():
        first = (i % blocks_per_seq) == 0
        halo = halo_ref[...]
        xs_ref[0:hr, :] = jnp.where(first, jnp.zeros_like(halo), halo)
        xs_ref[hr:, :] = x_ref[...]

    xs = xs_ref[...]
    hg_ref[...] = _dot(xs, wg_ref[...])
    hu_ref[...] = _dot(xs, wu_ref[...])

    def conv(h_ref, c_ref):
        c = c_ref[...]
        out = c[CONV_WIDTH:CONV_WIDTH + 1]
        for tap in range(CONV_WIDTH):
            shift = CONV_WIDTH - 1 - tap
            out = out + h_ref[pl.ds(hr - shift, tm), :] * c[tap:tap + 1]
        return out

    g = conv(hg_ref, cg_ref)
    u = conv(hu_ref, cu_ref)
    gelu = 0.5 * g * (1.0 + lax.erf(g * (0.5 ** 0.5)))
    act = (gelu * u).astype(BF16)
    part = _dot(act, wd_ref[...])

    @pl.when(j == 0)
    def _():
        o_ref[...] = part

    @pl.when(j > 0)
    def _():
        o_ref[...] += part


def _ffn(xb, wg, wu, convp, wd, seq_len, tm=512, tf=512):
    m, d = xb.shape
    dffp = wg.shape[1]
    nf = dffp // tf
    hr = BF16_SUBLANES
    kern = functools.partial(_ffn_kernel, tm=tm, blocks_per_seq=seq_len // tm)
    return pl.pallas_call(
        kern,
        grid=(m // tm, nf),
        in_specs=[pl.BlockSpec((tm, d), lambda i, j: (i, 0)),
                  pl.BlockSpec((hr, d), lambda i, j: (jnp.maximum(i * (tm // hr) - 1, 0), 0)),
                  pl.BlockSpec((d, tf), lambda i, j: (0, j)),
                  pl.BlockSpec((d, tf), lambda i, j: (0, j)),
                  pl.BlockSpec((8, tf), lambda i, j: (0, j)),
                  pl.BlockSpec((8, tf), lambda i, j: (0, nf + j)),
                  pl.BlockSpec((tf, d), lambda i, j: (j, 0))],
        out_specs=pl.BlockSpec((tm, d), lambda i, j: (i, 0)),
        out_shape=jax.ShapeDtypeStruct((m, d), F32),
        scratch_shapes=[pltpu.VMEM((tm + hr, d), BF16),
                        pltpu.VMEM((tm + hr, tf), F32),
                        pltpu.VMEM((tm + hr, tf), F32)],
        compiler_params=_params("parallel", "arbitrary"),
        name="conv_ffn",
    )(xb, xb, wg, wu, convp, convp, wd)


def _ple_ln_kernel(xb_ref, x_ref, f_ref, p_ref, wg_ref, wp_ref, g_ref, b_ref, of_ref, ob_ref):
    gate = jax.nn.sigmoid(_dot(xb_ref[...], wg_ref[...]))
    proj = _dot(p_ref[...].astype(BF16), wp_ref[...])
    y = _layer_norm(ALPHA * x_ref[...] + f_ref[...] + gate * proj, g_ref[...], b_ref[...])
    of_ref[...] = y
    ob_ref[...] = y.astype(BF16)


def _ple_ln(xb, x, f, p, wg, wp, g, b, tm=256):
    m, d = x.shape
    pd = p.shape[1]
    row = lambda i: (i, 0)
    const = lambda i: (0, 0)
    return pl.pallas_call(
        _ple_ln_kernel,
        grid=(m // tm,),
        in_specs=[pl.BlockSpec((tm, d), row), pl.BlockSpec((tm, d), row),
                  pl.BlockSpec((tm, d), row), pl.BlockSpec((tm, pd), row),
                  pl.BlockSpec((d, d), const), pl.BlockSpec((pd, d), const),
                  pl.BlockSpec((1, d), const), pl.BlockSpec((1, d), const)],
        out_specs=[pl.BlockSpec((tm, d), row), pl.BlockSpec((tm, d), row)],
        out_shape=[jax.ShapeDtypeStruct((m, d), F32), jax.ShapeDtypeStruct((m, d), BF16)],
        compiler_params=_params("parallel"),
        name="ple_ln",
    )(xb, x, f, p, wg, wp, g, b)


def _split3(a):
    hi = a.astype(BF16)
    r1 = a - hi.astype(F32)
    mid = r1.astype(BF16)
    lo = (r1 - mid.astype(F32)).astype(BF16)
    return hi, mid, lo


def _gla_kernel(q_ref, k_ref, v_ref, r_ref, t_ref, wa2_ref, ba_ref, ng_ref, o_ref, s_ref, *, chunk):
    c = pl.program_id(2)
    dk = q_ref.shape[1]

    @pl.when(c == 0)
    def _():
        s_ref[...] = jnp.zeros_like(s_ref)

    z = _dot(t_ref[...].astype(BF16), wa2_ref[...]) + ba_ref[...]
    la = (jnp.minimum(z, 0.0) - jnp.log1p(jnp.exp(-jnp.abs(z)))) * (1.0 / GLA_TAU)
    row = lax.broadcasted_iota(jnp.int32, (chunk, chunk), 0)
    col = lax.broadcasted_iota(jnp.int32, (chunk, chunk), 1)
    causal = row >= col
    tri = jnp.where(causal, 1.0, 0.0).astype(BF16)
    hi, mid, lo = _split3(la)
    b = _dot(tri, hi) + _dot(tri, mid) + _dot(tri, lo)

    eb = jnp.exp(b)
    qt = (q_ref[...].astype(F32) * (dk ** -0.5) * eb).astype(BF16)
    kt = k_ref[...].astype(F32) * jnp.exp(-b)
    v = v_ref[...]
    state = s_ref[...]

    o = _dot(qt, state.astype(BF16))
    scores = jnp.where(causal, _dot_nt(qt, kt.astype(BF16)), 0.0).astype(BF16)
    o = o + _dot(scores, v)

    eb_last = eb[chunk - LANES:, :].T[:, LANES - 1:LANES]
    s_ref[...] = (state + _dot(kt.T.astype(BF16), v)) * eb_last

    o = o * lax.rsqrt(jnp.mean(o * o, -1, keepdims=True) + LN_EPS) * ng_ref[...]
    r = r_ref[...].astype(F32)
    o_ref[...] = (o * (r * jax.nn.sigmoid(r))).astype(BF16)


def _gla(qkvr, t, wa2, ba, ng, batch, seq_len, chunk=256):
    m = qkvr.shape[0]
    h = GLA_HEADS
    dk = wa2.shape[1] // h
    dv = ng.shape[1] // h
    nc = seq_len // chunk
    rows = lambda b, hh, c: b * nc + c
    kern = functools.partial(_gla_kernel, chunk=chunk)
    return pl.pallas_call(
        kern,
        grid=(batch, h, nc),
        in_specs=[pl.BlockSpec((chunk, dk), lambda b, hh, c: (rows(b, hh, c), hh)),
                  pl.BlockSpec((chunk, dk), lambda b, hh, c: (rows(b, hh, c), h + hh)),
                  pl.BlockSpec((chunk, dv), lambda b, hh, c: (rows(b, hh, c), (2 * h * dk) // dv + hh)),
                  pl.BlockSpec((chunk, dv), lambda b, hh, c: (rows(b, hh, c), (2 * h * dk) // dv + h + hh)),
                  pl.BlockSpec((chunk, LANES), lambda b, hh, c: (rows(b, hh, c), 0)),
                  pl.BlockSpec((LANES, dk), lambda b, hh, c: (0, hh)),
                  pl.BlockSpec((1, dk), lambda b, hh, c: (0, hh)),
                  pl.BlockSpec((1, dv), lambda b, hh, c: (0, hh))],
        out_specs=pl.BlockSpec((chunk, dv), lambda b, hh, c: (rows(b, hh, c), hh)),
        out_shape=jax.ShapeDtypeStruct((m, h * dv), BF16),
        scratch_shapes=[pltpu.VMEM((dk, dv), F32)],
        compiler_params=_params("parallel", "parallel", "arbitrary"),
        name="gla_chunk",
    )(qkvr, qkvr, qkvr, qkvr, t, wa2, ba, ng)


def _pool_kernel(x_ref, halo_ref, w_ref, sc_ref, g_ref, b_ref, of_ref, ob_ref, xe_ref,
                 *, tm, blocks_per_seq, halo_rows):
    i = pl.program_id(0)
    blk = i % blocks_per_seq
    halo = halo_ref[...]
    xe_ref[0:halo_rows, :] = jnp.where(blk == 0, jnp.zeros_like(halo), halo)
    xe_ref[halo_rows:, :] = x_ref[...]
    gw = w_ref.shape[1]
    pos = blk * tm + lax.broadcasted_iota(jnp.int32, (tm, 1), 0)
    ys = []
    for g, w in enumerate(POOL_WINDOWS):
        cols = slice(g * gw, (g + 1) * gw)
        xg = xe_ref[pl.ds(halo_rows, tm), cols]
        acc = xg
        for dlt in range(1, w):
            acc = acc + xe_ref[pl.ds(halo_rows - dlt, tm), cols]
        cnt = jnp.minimum(pos + 1, w).astype(F32)
        pooled = acc / cnt - xg
        ys.append(_dot(pooled.astype(BF16), w_ref[g]))
    y = jnp.concatenate(ys, axis=-1) * sc_ref[...]
    out = _layer_norm(ALPHA * x_ref[...] + y, g_ref[...], b_ref[...])
    of_ref[...] = out
    ob_ref[...] = out.astype(BF16)


def _pool(x, w, scale, g, b, seq_len, tm=256):
    m, d = x.shape
    halo_rows = max(POOL_WINDOWS)
    kern = functools.partial(_pool_kernel, tm=tm, blocks_per_seq=seq_len // tm, halo_rows=halo_rows)
    row = lambda i: (i, 0)
    const = lambda i: (0, 0)
    return pl.pallas_call(
        kern,
        grid=(m // tm,),
        in_specs=[pl.BlockSpec((tm, d), row),
                  pl.BlockSpec((halo_rows, d), lambda i: (jnp.maximum(i * (tm // halo_rows) - 1, 0), 0)),
                  pl.BlockSpec(w.shape, lambda i: (0, 0, 0)),
                  pl.BlockSpec((1, d), const), pl.BlockSpec((1, d), const), pl.BlockSpec((1, d), const)],
        out_specs=[pl.BlockSpec((tm, d), row), pl.BlockSpec((tm, d), row)],
        out_shape=[jax.ShapeDtypeStruct((m, d), F32), jax.ShapeDtypeStruct((m, d), BF16)],
        scratch_shapes=[pltpu.VMEM((tm + halo_rows, d), F32)],
        compiler_params=_params("parallel"),
        name="pool_mixer",
    )(x, x, w, scale, g, b)


def _ki_prep_kernel(kw_ref, g_ref, b_ref, o_ref):
    x = kw_ref[...]
    lane = lax.broadcasted_iota(jnp.int32, x.shape, 1)
    isk = lane < DSA_IDX_DIM
    mu = jnp.sum(jnp.where(isk, x, 0.0), -1, keepdims=True) * (1.0 / DSA_IDX_DIM)
    xc = jnp.where(isk, x - mu, 0.0)
    var = jnp.sum(xc * xc, -1, keepdims=True) * (1.0 / DSA_IDX_DIM)
    kn = jnp.where(isk, xc * lax.rsqrt(var + LN_EPS) * g_ref[...] + b_ref[...], 0.0)
    o_ref[:, 0:LANES] = kn.astype(BF16)
    o_ref[:, LANES:2 * LANES] = pltpu.roll(kn, DSA_IDX_DIM, axis=1).astype(BF16)


def _ki_prep(kiwi, g, b, tm=1024):
    m = kiwi.shape[0]
    return pl.pallas_call(
        _ki_prep_kernel,
        grid=(m // tm,),
        in_specs=[pl.BlockSpec((tm, LANES), lambda i: (i, 0)),
                  pl.BlockSpec((1, LANES), lambda i: (0, 0)),
                  pl.BlockSpec((1, LANES), lambda i: (0, 0))],
        out_specs=pl.BlockSpec((tm, 2 * LANES), lambda i: (i, 0)),
        out_shape=jax.ShapeDtypeStruct((m, 2 * LANES), BF16),
        compiler_params=_params("parallel"),
        name="dsa_ki_prep",
    )(kiwi, g, b)


def _key_to_float(key):
    bits = jnp.where(key < 0, key ^ 0x7FFFFFFF, key)
    return lax.bitcast_convert_type(bits, F32)


def _dsa_index_kernel(qi_ref, kw_ref, kk_ref, o_ref, sc_ref, *, tq, tk, topk):
    i = pl.program_id(1)
    nk = sc_ref.shape[0]
    t0 = i * tq
    n_chunks = (t0 + tq + tk - 1) // tk
    wscale = (DSA_IDX_HEADS ** -0.5) * (DSA_IDX_DIM ** -0.5)
    wi = kw_ref[:, DSA_IDX_DIM:DSA_IDX_DIM + DSA_IDX_HEADS] * wscale
    tpos = t0 + lax.broadcasted_iota(jnp.int32, (tq, tk), 0)
    scol = lax.broadcasted_iota(jnp.int32, (tq, tk), 1)

    def score_chunk(c, carry):
        start = pl.multiple_of(c * tk, tk)
        ka = kk_ref[pl.ds(start, tk), 0:LANES]
        kb = kk_ref[pl.ds(start, tk), LANES:2 * LANES]
        acc = jnp.zeros((tq, tk), F32)
        for pair in range(DSA_IDX_HEADS // 2):
            qp = qi_ref[:, pair * LANES:(pair + 1) * LANES]
            h0 = 2 * pair
            acc = acc + jnp.maximum(_dot_nt(qp, ka), 0.0) * wi[:, h0:h0 + 1]
            acc = acc + jnp.maximum(_dot_nt(qp, kb), 0.0) * wi[:, h0 + 1:h0 + 2]
        sc_ref[c] = jnp.where(c * tk + scol <= tpos, acc, -jnp.inf)
        return carry

    lax.fori_loop(0, n_chunks, score_chunk, 0)

    def count_ge(thr):
        def body(c, cnt):
            s = sc_ref[c]
            for lt in range(tk // LANES):
                cnt = cnt + jnp.where(s[:, lt * LANES:(lt + 1) * LANES] >= thr, 1.0, 0.0)
            return cnt
        cnt = lax.fori_loop(0, n_chunks, body, jnp.zeros((tq, LANES), F32))
        return jnp.sum(cnt, -1, keepdims=True)

    def bisect(step, key):
        cand = key + jnp.left_shift(jnp.int32(1), 31 - step)
        ok = count_ge(_key_to_float(cand)) >= float(topk)
        return jnp.where(ok, cand, key)

    key = lax.fori_loop(0, 32, bisect, jnp.full((tq, 1), INT_MIN, jnp.int32))
    thr = _key_to_float(jnp.maximum(key, KEY_NEG_FLT_MAX))

    def write_chunk(c, carry):
        o_ref[0, c] = jnp.where(sc_ref[c] >= thr, 0.0, NEG_BIG).astype(BF16)
        return carry

    def fill_chunk(c, carry):
        o_ref[0, c] = jnp.full((tq, tk), NEG_BIG, BF16)
        return carry

    lax.fori_loop(0, n_chunks, write_chunk, 0)
    lax.fori_loop(n_chunks, nk, fill_chunk, 0)


def _dsa_index(proj, kiwi, kk, batch, seq_len, qi_col_block, tq=128, tk=512):
    nq = seq_len // tq
    nk = seq_len // tk
    topk = min(DSA_TOPK_MAX, seq_len // 4)
    qi_w = DSA_IDX_HEADS * DSA_IDX_DIM
    kern = functools.partial(_dsa_index_kernel, tq=tq, tk=tk, topk=topk)
    return pl.pallas_call(
        kern,
        grid=(batch, nq),
        in_specs=[pl.BlockSpec((tq, qi_w), lambda b, i: (b * nq + i, qi_col_block)),
                  pl.BlockSpec((tq, LANES), lambda b, i: (b * nq + i, 0)),
                  pl.BlockSpec((seq_len, 2 * LANES), lambda b, i: (b, 0))],
        out_specs=pl.BlockSpec((1, nk, tq, tk), lambda b, i: (b * nq + i, 0, 0, 0)),
        out_shape=jax.ShapeDtypeStruct((batch * nq, nk, tq, tk), BF16),
        scratch_shapes=[pltpu.VMEM((nk, tq, tk), F32)],
        compiler_params=_params("parallel", "parallel"),
        name="dsa_index_topk",
    )(proj, kiwi, kk)


def _dsa_attn_kernel(q_ref, k_ref, v_ref, bias_ref, o_ref, qs_ref, m_ref, l_ref, acc_ref, *, tq, tk):
    i = pl.program_id(1)
    j = pl.program_id(2)
    groups = DSA_KV_HEADS
    rep = DSA_HEADS // DSA_KV_HEADS
    dh = k_ref.shape[1] // groups
    last = ((i + 1) * tq - 1) // tk

    @pl.when(j == 0)
    def _():
        m_ref[...] = jnp.full_like(m_ref, NEG_BIG)
        l_ref[...] = jnp.zeros_like(l_ref)
        acc_ref[...] = jnp.zeros_like(acc_ref)
        for g in range(groups):
            for r in range(rep):
                hd = g * rep + r
                qh = q_ref[:, hd * dh:(hd + 1) * dh].astype(F32) * (dh ** -0.5)
                qs_ref[g, r * tq:(r + 1) * tq, :] = qh.astype(BF16)

    @pl.when(j <= last)
    def _():
        bias = bias_ref[...].reshape(tq, tk).astype(F32)
        for g in range(groups):
            kg = k_ref[:, g * dh:(g + 1) * dh]
            vg = v_ref[:, g * dh:(g + 1) * dh]
            s = _dot_nt(qs_ref[g], kg)
            s = (s.reshape(rep, tq, tk) + bias[None]).reshape(rep * tq, tk)
            m_prev = m_ref[g]
            m_new = jnp.maximum(m_prev, jnp.max(s, -1, keepdims=True))
            alpha = jnp.exp(m_prev - m_new)
            p = jnp.exp(s - m_new)
            l_ref[g] = alpha * l_ref[g] + jnp.sum(p, -1, keepdims=True)
            acc_ref[g] = alpha * acc_ref[g] + _dot(p.astype(BF16), vg)
            m_ref[g] = m_new

    @pl.when(j == last)
    def _():
        for g in range(groups):
            og = acc_ref[g] / l_ref[g]
            for r in range(rep):
                hd = g * rep + r
                o_ref[:, hd * dh:(hd + 1) * dh] = og[r * tq:(r + 1) * tq, :].astype(BF16)


def _dsa_attn(proj, bias, batch, seq_len, dh, tq_index, tq=256, tk=512):
    m = proj.shape[0]
    nq = seq_len // tq
    nk = seq_len // tk
    dq = DSA_HEADS * dh
    dkv = DSA_KV_HEADS * dh
    sub = tq // tq_index
    rep = DSA_HEADS // DSA_KV_HEADS
    last = lambda i: ((i + 1) * tq - 1) // tk
    kern = functools.partial(_dsa_attn_kernel, tq=tq, tk=tk)
    return pl.pallas_call(
        kern,
        grid=(batch, nq, nk),
        in_specs=[pl.BlockSpec((tq, dq), lambda b, i, j: (b * nq + i, 0)),
                  pl.BlockSpec((tk, dkv), lambda b, i, j: (b * nk + jnp.minimum(j, last(i)), dq // dkv)),
                  pl.BlockSpec((tk, dkv), lambda b, i, j: (b * nk + jnp.minimum(j, last(i)), dq // dkv + 1)),
                  pl.BlockSpec((sub, 1, tq_index, tk),
                               lambda b, i, j: (b * nq + i, jnp.minimum(j, last(i)), 0, 0))],
        out_specs=pl.BlockSpec((tq, dq), lambda b, i, j: (b * nq + i, 0)),
        out_shape=jax.ShapeDtypeStruct((m, dq), BF16),
        scratch_shapes=[pltpu.VMEM((DSA_KV_HEADS, rep * tq, dh), BF16),
                        pltpu.VMEM((DSA_KV_HEADS, rep * tq, 1), F32),
                        pltpu.VMEM((DSA_KV_HEADS, rep * tq, 1), F32),
                        pltpu.VMEM((DSA_KV_HEADS, rep * tq, dh), F32)],
        compiler_params=_params("parallel", "parallel", "arbitrary"),
        name="dsa_attention",
    )(proj, proj, proj, bias)


def _pad_cols(w, n):
    return jnp.pad(w, ((0, 0), (0, n - w.shape[1])))


def _gla_mixer(xb, batch, seq_len, w_in, w_a1, w_a2, b_a, norm_g):
    qkvr = _matmul(xb, w_in.astype(BF16), BF16, tm=1024, tn=512)
    t = _matmul(xb, _pad_cols(w_a1, LANES).astype(BF16), F32, tm=1024, tn=LANES)
    wa2 = jnp.pad(w_a2, ((0, LANES - w_a2.shape[0]), (0, 0))).astype(BF16)
    return _gla(qkvr, t, wa2, b_a[None, :], norm_g[None, :], batch, seq_len)


def _dsa_mixer(xb, batch, seq_len, w_in, kidx_g, kidx_b):
    d = w_in.shape[0]
    dh = d // DSA_HEADS
    main = d + 2 * DSA_KV_HEADS * dh + DSA_IDX_HEADS * DSA_IDX_DIM
    proj = _matmul(xb, w_in[:, :main].astype(BF16), BF16, tm=1024, tn=512)
    kiwi = _matmul(xb, _pad_cols(w_in[:, main:], LANES).astype(BF16), F32, tm=1024, tn=LANES)
    pad1 = lambda a: jnp.pad(a, (0, LANES - a.shape[0]))[None, :]
    kk = _ki_prep(kiwi, pad1(kidx_g), pad1(kidx_b))
    qi_w = DSA_IDX_HEADS * DSA_IDX_DIM
    tq_index = 128
    bias = _dsa_index(proj, kiwi, kk, batch, seq_len, (main - qi_w) // qi_w, tq=tq_index)
    return _dsa_attn(proj, bias, batch, seq_len, dh, tq_index)


def kernel(x, p, gla_w_in, gla_w_a1, gla_w_a2, gla_b_a, gla_norm_g, gla_w_o, pool_w, pool_scale,
           dsa_w_in, dsa_kidx_g, dsa_kidx_b, dsa_w_o, ln_mix_g, ln_mix_b, ffn_w_up, ffn_conv_w,
           ffn_conv_b, ffn_w_down, ple_gate_w, ple_proj_w, ln_ffn_g, ln_ffn_b):
    batch, seq_len, d = x.shape
    m = batch * seq_len
    dff = ffn_w_down.shape[1]
    tf = 512
    dffp = -(-dff // tf) * tf
    xf = x.reshape(m, d)
    xb = xf.astype(BF16)
    for i in range(DEPTH):
        kind, j = i % N_MIXERS, i // N_MIXERS
        g_mix, b_mix = ln_mix_g[i][None, :], ln_mix_b[i][None, :]
        if kind == 0:
            o = _gla_mixer(xb, batch, seq_len, gla_w_in[j], gla_w_a1[j], gla_w_a2[j], gla_b_a[j],
                           gla_norm_g[j])
            xf, xb = _proj_res_ln(o, gla_w_o[j].astype(BF16), xf, g_mix, b_mix)
        elif kind == 1:
            xf, xb = _pool(xf, pool_w[j].astype(BF16), pool_scale[j][None, :], g_mix, b_mix, seq_len)
        else:
            o = _dsa_mixer(xb, batch, seq_len, dsa_w_in[j], dsa_kidx_g[j], dsa_kidx_b[j])
            xf, xb = _proj_res_ln(o, dsa_w_o[j].astype(BF16), xf, g_mix, b_mix)

        w_up = ffn_w_up[i]
        wg = _pad_cols(w_up[:, :dff], dffp).astype(BF16)
        wu = _pad_cols(w_up[:, dff:], dffp).astype(BF16)
        cw, cb = ffn_conv_w[i], ffn_conv_b[i]
        rows_g = jnp.concatenate([cw[:, :dff], cb[None, :dff]], 0)
        rows_u = jnp.concatenate([cw[:, dff:], cb[None, dff:]], 0)
        convp = jnp.concatenate([_pad_cols(rows_g, dffp), _pad_cols(rows_u, dffp)], 1)
        convp = jnp.pad(convp, ((0, 8 - convp.shape[0]), (0, 0)))
        wd = jnp.pad(ffn_w_down[i], ((0, dffp - dff), (0, 0))).astype(BF16)
        f = _ffn(xb, wg, wu, convp, wd, seq_len, tf=tf)
        xf, xb = _ple_ln(xb, xf, f, p[i].reshape(m, -1), ple_gate_w[i].astype(BF16),
                         ple_proj_w[i].astype(BF16), ln_ffn_g[i][None, :], ln_ffn_b[i][None, :])
    return xf.reshape(batch, seq_len, d)
```

```python
import functools

import jax
import jax.numpy as jnp
from jax import lax
from jax.experimental import pallas as pl
from jax.experimental.pallas import tpu as pltpu

F32 = jnp.float32
BF16 = jnp.bfloat16

DEPTH = 4
N_MIXERS = 3
LN_EPS = 1e-5
ALPHA = (2.0 * DEPTH) ** 0.25
GLA_HEADS = 4
GLA_TAU = 16.0
POOL_WINDOWS = (2, 4, 8, 16)
DSA_HEADS = 16
DSA_KV_HEADS = 4
DSA_IDX_HEADS = 16
DSA_IDX_DIM = 64
DSA_TOPK_MAX = 256
CONV_WIDTH = 3

LANES = 128
BF16_SUBLANES = 16
VMEM_LIMIT_BYTES = 56 * 1024 * 1024

NEG_BIG = -1e30
LOG2_E = 1.4426950408889634
INT_MIN = -(2 ** 31)
KEY_NEG_FLT_MAX = INT_MIN + 0x800000


def _params(*sem):
    return pltpu.CompilerParams(dimension_semantics=sem, vmem_limit_bytes=VMEM_LIMIT_BYTES)


def _dot(a, b):
    return jnp.dot(a, b, preferred_element_type=F32)


def _dot_nt(a, b):
    return lax.dot_general(a, b, (((1,), (1,)), ((), ())), preferred_element_type=F32)


def _layer_norm(y, g, b):
    mu = jnp.mean(y, -1, keepdims=True)
    yc = y - mu
    var = jnp.mean(yc * yc, -1, keepdims=True)
    return yc * lax.rsqrt(var + LN_EPS) * g + b


def _matmul_kernel(x_ref, w_ref, o_ref):
    o_ref[...] = _dot(x_ref[...], w_ref[...]).astype(o_ref.dtype)


def _matmul(x, w, out_dtype, tm, tn):
    m, k = x.shape
    n = w.shape[1]
    return pl.pallas_call(
        _matmul_kernel,
        grid=(m // tm, n // tn),
        in_specs=[pl.BlockSpec((tm, k), lambda i, j: (i, 0)),
                  pl.BlockSpec((k, tn), lambda i, j: (0, j))],
        out_specs=pl.BlockSpec((tm, tn), lambda i, j: (i, j)),
        out_shape=jax.ShapeDtypeStruct((m, n), out_dtype),
        compiler_params=_params("parallel", "parallel"),
        name="matmul",
    )(x, w)


def _proj_res_ln_kernel(a_ref, w_ref, x_ref, g_ref, b_ref, of_ref, ob_ref):
    m = _dot(a_ref[...], w_ref[...])
    y = _layer_norm(ALPHA * x_ref[...] + m, g_ref[...], b_ref[...])
    of_ref[...] = y
    ob_ref[...] = y.astype(BF16)


def _proj_res_ln(a, w, x, g, b, tm=256):
    m, ka = a.shape
    d = w.shape[1]
    row = lambda i: (i, 0)
    const = lambda i: (0, 0)
    return pl.pallas_call(
        _proj_res_ln_kernel,
        grid=(m // tm,),
        in_specs=[pl.BlockSpec((tm, ka), row), pl.BlockSpec((ka, d), const),
                  pl.BlockSpec((tm, d), row), pl.BlockSpec((1, d), const),
                  pl.BlockSpec((1, d), const)],
        out_specs=[pl.BlockSpec((tm, d), row), pl.BlockSpec((tm, d), row)],
        out_shape=[jax.ShapeDtypeStruct((m, d), F32), jax.ShapeDtypeStruct((m, d), BF16)],
        compiler_params=_params("parallel"),
        name="proj_res_ln",
    )(a, w, x, g, b)


def _ffn_kernel(x_ref, halo_ref, wg_ref, wu_ref, cg_ref, cu_ref, wd_ref, o_ref,
                xs_ref, hg_ref, hu_ref, *, tm, blocks_per_seq):
    i = pl.program_id(0)
    j = pl.program_id(1)
    hr = BF16_SUBLANES

    @pl.when(j == 0)
    def _():
        first = (i % blocks_per_seq) == 0
        halo = halo_ref[...]
        xs_ref[0:hr, :] = jnp.where(first, jnp.zeros_like(halo), halo)
        xs_ref[hr:, :] = x_ref[...]

    xs = xs_ref[...]
    hg_ref[...] = _dot(xs, wg_ref[...])
    hu_ref[...] = _dot(xs, wu_ref[...])

    def conv(h_ref, c_ref):
        c = c_ref[...]
        out = c[CONV_WIDTH:CONV_WIDTH + 1]
        for tap in range(CONV_WIDTH):
            shift = CONV_WIDTH - 1 - tap
            out = out + h_ref[pl.ds(hr - shift, tm), :] * c[tap:tap + 1]
        return out

    g = conv(hg_ref, cg_ref)
    u = conv(hu_ref, cu_ref)
    gelu = 0.5 * g * (1.0 + lax.erf(g * (0.5 ** 0.5)))
    act = (gelu * u).astype(BF16)
    part = _dot(act, wd_ref[...])

    @pl.when(j == 0)
    def _():
        o_ref[...] = part

    @pl.when(j > 0)
    def _():
        o_ref[...] += part


def _ffn(xb, wg, wu, convp, wd, seq_len, tm=512, tf=512):
    m, d = xb.shape
    dffp = wg.shape[1]
    nf = dffp // tf
    hr = BF16_SUBLANES
    kern = functools.partial(_ffn_kernel, tm=tm, blocks_per_seq=seq_len // tm)
    return pl.pallas_call(
        kern,
        grid=(m // tm, nf),
        in_specs=[pl.BlockSpec((tm, d), lambda i, j: (i, 0)),
                  pl.BlockSpec((hr, d), lambda i, j: (jnp.maximum(i * (tm // hr) - 1, 0), 0)),
                  pl.BlockSpec((d, tf), lambda i, j: (0, j)),
                  pl.BlockSpec((d, tf), lambda i, j: (0, j)),
                  pl.BlockSpec((8, tf), lambda i, j: (0, j)),
                  pl.BlockSpec((8, tf), lambda i, j: (0, nf + j)),
                  pl.BlockSpec((tf, d), lambda i, j: (j, 0))],
        out_specs=pl.BlockSpec((tm, d), lambda i, j: (i, 0)),
        out_shape=jax.ShapeDtypeStruct((m, d), F32),
        scratch_shapes=[pltpu.VMEM((tm + hr, d), BF16),
                        pltpu.VMEM((tm + hr, tf), F32),
                        pltpu.VMEM((tm + hr, tf), F32)],
        compiler_params=_params("parallel", "arbitrary"),
        name="conv_ffn",
    )(xb, xb, wg, wu, convp, convp, wd)


def _ple_ln_kernel(xb_ref, x_ref, f_ref, p_ref, wg_ref, wp_ref, g_ref, b_ref, of_ref, ob_ref):
    gate = jax.nn.sigmoid(_dot(xb_ref[...], wg_ref[...]))
    proj = _dot(p_ref[...].astype(BF16), wp_ref[...])
    y = _layer_norm(ALPHA * x_ref[...] + f_ref[...] + gate * proj, g_ref[...], b_ref[...])
    of_ref[...] = y
    ob_ref[...] = y.astype(BF16)


def _ple_ln(xb, x, f, p, wg, wp, g, b, tm=256):
    m, d = x.shape
    pd = p.shape[1]
    row = lambda i: (i, 0)
    const = lambda i: (0, 0)
    return pl.pallas_call(
        _ple_ln_kernel,
        grid=(m // tm,),
        in_specs=[pl.BlockSpec((tm, d), row), pl.BlockSpec((tm, d), row),
                  pl.BlockSpec((tm, d), row), pl.BlockSpec((tm, pd), row),
                  pl.BlockSpec((d, d), const), pl.BlockSpec((pd, d), const),
                  pl.BlockSpec((1, d), const), pl.BlockSpec((1, d), const)],
        out_specs=[pl.BlockSpec((tm, d), row), pl.BlockSpec((tm, d), row)],
        out_shape=[jax.ShapeDtypeStruct((m, d), F32), jax.ShapeDtypeStruct((m, d), BF16)],
        compiler_params=_params("parallel"),
        name="ple_ln",
    )(xb, x, f, p, wg, wp, g, b)


def _split3(a):
    hi = a.astype(BF16)
    r1 = a - hi.astype(F32)
    mid = r1.astype(BF16)
    lo = (r1 - mid.astype(F32)).astype(BF16)
    return hi, mid, lo


def _gla_kernel(q_ref, k_ref, v_ref, r_ref, t_ref, wa2_ref, ba_ref, ng_ref, o_ref, s_ref, *, chunk):
    c = pl.program_id(2)
    dk = q_ref.shape[1]

    @pl.when(c == 0)
    def _():
        s_ref[...] = jnp.zeros_like(s_ref)

    z = _dot(t_ref[...].astype(BF16), wa2_ref[...]) + ba_ref[...]
    la = (jnp.minimum(z, 0.0) - jnp.log1p(jnp.exp(-jnp.abs(z)))) * (1.0 / GLA_TAU)
    row = lax.broadcasted_iota(jnp.int32, (chunk, chunk), 0)
    col = lax.broadcasted_iota(jnp.int32, (chunk, chunk), 1)
    causal = row >= col
    tri = jnp.where(causal, 1.0, 0.0).astype(BF16)
    hi, mid, lo = _split3(la)
    b = _dot(tri, hi) + _dot(tri, mid) + _dot(tri, lo)

    eb = jnp.exp(b)
    qt = (q_ref[...].astype(F32) * (dk ** -0.5) * eb).astype(BF16)
    kt = k_ref[...].astype(F32) * jnp.exp(-b)
    v = v_ref[...]
    state = s_ref[...]

    o = _dot(qt, state.astype(BF16))
    scores = jnp.where(causal, _dot_nt(qt, kt.astype(BF16)), 0.0).astype(BF16)
    o = o + _dot(scores, v)

    eb_last = eb[chunk - LANES:, :].T[:, LANES - 1:LANES]
    s_ref[...] = (state + _dot(kt.T.astype(BF16), v)) * eb_last

    o = o * lax.rsqrt(jnp.mean(o * o, -1, keepdims=True) + LN_EPS) * ng_ref[...]
    r = r_ref[...].astype(F32)
    o_ref[...] = (o * (r * jax.nn.sigmoid(r))).astype(BF16)


def _gla(qkvr, t, wa2, ba, ng, batch, seq_len, chunk=256):
    m = qkvr.shape[0]
    h = GLA_HEADS
    dk = wa2.shape[1] // h
    dv = ng.shape[1] // h
    nc = seq_len // chunk
    rows = lambda b, hh, c: b * nc + c
    kern = functools.partial(_gla_kernel, chunk=chunk)
    return pl.pallas_call(
        kern,
        grid=(batch, h, nc),
        in_specs=[pl.BlockSpec((chunk, dk), lambda b, hh, c: (rows(b, hh, c), hh)),
                  pl.BlockSpec((chunk, dk), lambda b, hh, c: (rows(b, hh, c), h + hh)),
                  pl.BlockSpec((chunk, dv), lambda b, hh, c: (rows(b, hh, c), (2 * h * dk) // dv + hh)),
                  pl.BlockSpec((chunk, dv), lambda b, hh, c: (rows(b, hh, c), (2 * h * dk) // dv + h + hh)),
                  pl.BlockSpec((chunk, LANES), lambda b, hh, c: (rows(b, hh, c), 0)),
                  pl.BlockSpec((LANES, dk), lambda b, hh, c: (0, hh)),
                  pl.BlockSpec((1, dk), lambda b, hh, c: (0, hh)),
                  pl.BlockSpec((1, dv), lambda b, hh, c: (0, hh))],
        out_specs=pl.BlockSpec((chunk, dv), lambda b, hh, c: (rows(b, hh, c), hh)),
        out_shape=jax.ShapeDtypeStruct((m, h * dv), BF16),
        scratch_shapes=[pltpu.VMEM((dk, dv), F32)],
        compiler_params=_params("parallel", "parallel", "arbitrary"),
        name="gla_chunk",
    )(qkvr, qkvr, qkvr, qkvr, t, wa2, ba, ng)


def _pool_kernel(x_ref, halo_ref, w_ref, sc_ref, g_ref, b_ref, of_ref, ob_ref, xe_ref,
                 *, tm, blocks_per_seq, halo_rows):
    i = pl.program_id(0)
    blk = i % blocks_per_seq
    halo = halo_ref[...]
    xe_ref[0:halo_rows, :] = jnp.where(blk == 0, jnp.zeros_like(halo), halo)
    xe_ref[halo_rows:, :] = x_ref[...]
    gw = w_ref.shape[1]
    pos = blk * tm + lax.broadcasted_iota(jnp.int32, (tm, 1), 0)
    ys = []
    for g, w in enumerate(POOL_WINDOWS):
        cols = slice(g * gw, (g + 1) * gw)
        xg = xe_ref[pl.ds(halo_rows, tm), cols]
        acc = xg
        for dlt in range(1, w):
            acc = acc + xe_ref[pl.ds(halo_rows - dlt, tm), cols]
        cnt = jnp.minimum(pos + 1, w).astype(F32)
        pooled = acc / cnt - xg
        ys.append(_dot(pooled.astype(BF16), w_ref[g]))
    y = jnp.concatenate(ys, axis=-1) * sc_ref[...]
    out = _layer_norm(ALPHA * x_ref[...] + y, g_ref[...], b_ref[...])
    of_ref[...] = out
    ob_ref[...] = out.astype(BF16)


def _pool(x, w, scale, g, b, seq_len, tm=256):
    m, d = x.shape
    halo_rows = max(POOL_WINDOWS)
    kern = functools.partial(_pool_kernel, tm=tm, blocks_per_seq=seq_len // tm, halo_rows=halo_rows)
    row = lambda i: (i, 0)
    const = lambda i: (0, 0)
    return pl.pallas_call(
        kern,
        grid=(m // tm,),
        in_specs=[pl.BlockSpec((tm, d), row),
                  pl.BlockSpec((halo_rows, d), lambda i: (jnp.maximum(i * (tm // halo_rows) - 1, 0), 0)),
                  pl.BlockSpec(w.shape, lambda i: (0, 0, 0)),
                  pl.BlockSpec((1, d), const), pl.BlockSpec((1, d), const), pl.BlockSpec((1, d), const)],
        out_specs=[pl.BlockSpec((tm, d), row), pl.BlockSpec((tm, d), row)],
        out_shape=[jax.ShapeDtypeStruct((m, d), F32), jax.ShapeDtypeStruct((m, d), BF16)],
        scratch_shapes=[pltpu.VMEM((tm + halo_rows, d), F32)],
        compiler_params=_params("parallel"),
        name="pool_mixer",
    )(x, x, w, scale, g, b)


def _ki_prep_kernel(kw_ref, g_ref, b_ref, o_ref, wt_ref):
    x = kw_ref[...]
    wt_ref[...] = x.T * ((DSA_IDX_HEADS ** -0.5) * (DSA_IDX_DIM ** -0.5))
    lane = lax.broadcasted_iota(jnp.int32, x.shape, 1)
    isk = lane < DSA_IDX_DIM
    mu = jnp.sum(jnp.where(isk, x, 0.0), -1, keepdims=True) * (1.0 / DSA_IDX_DIM)
    xc = jnp.where(isk, x - mu, 0.0)
    var = jnp.sum(xc * xc, -1, keepdims=True) * (1.0 / DSA_IDX_DIM)
    kn = jnp.where(isk, xc * lax.rsqrt(var + LN_EPS) * g_ref[...] + b_ref[...], 0.0)
    o_ref[:, 0:LANES] = kn.astype(BF16)
    o_ref[:, LANES:2 * LANES] = pltpu.roll(kn, DSA_IDX_DIM, axis=1).astype(BF16)


def _ki_prep(kiwi, g, b, tm=1024):
    m = kiwi.shape[0]
    return pl.pallas_call(
        _ki_prep_kernel,
        grid=(m // tm,),
        in_specs=[pl.BlockSpec((tm, LANES), lambda i: (i, 0)),
                  pl.BlockSpec((1, LANES), lambda i: (0, 0)),
                  pl.BlockSpec((1, LANES), lambda i: (0, 0))],
        out_specs=[pl.BlockSpec((tm, 2 * LANES), lambda i: (i, 0)),
                   pl.BlockSpec((LANES, tm), lambda i: (0, i))],
        out_shape=[jax.ShapeDtypeStruct((m, 2 * LANES), BF16),
                   jax.ShapeDtypeStruct((LANES, m), F32)],
        compiler_params=_params("parallel"),
        name="dsa_ki_prep",
    )(kiwi, g, b)


def _key_to_float(key):
    bits = jnp.where(key < 0, key ^ 0x7FFFFFFF, key)
    return lax.bitcast_convert_type(bits, F32)


def _dsa_index_kernel(qi_ref, wt_ref, kk_ref, o_ref, sc_ref, *, tq, tk, topk):
    i = pl.program_id(1)
    nk = sc_ref.shape[0]
    t0 = i * tq
    n_chunks = (t0 + tq + tk - 1) // tk
    spos = lax.broadcasted_iota(jnp.int32, (tk, tq), 0)
    tpos = t0 + lax.broadcasted_iota(jnp.int32, (tk, tq), 1)

    def score_chunk(c, carry):
        start = pl.multiple_of(c * tk, tk)
        ka = kk_ref[pl.ds(start, tk), 0:LANES]
        kb = kk_ref[pl.ds(start, tk), LANES:2 * LANES]
        acc = jnp.zeros((tk, tq), F32)
        for pair in range(DSA_IDX_HEADS // 2):
            qp = qi_ref[:, pair * LANES:(pair + 1) * LANES]
            row = DSA_IDX_DIM + 2 * pair
            acc = acc + jnp.maximum(_dot_nt(ka, qp), 0.0) * wt_ref[row:row + 1, :]
            acc = acc + jnp.maximum(_dot_nt(kb, qp), 0.0) * wt_ref[row + 1:row + 2, :]
        sc_ref[c] = jnp.where(c * tk + spos <= tpos, acc, -jnp.inf)
        return carry

    lax.fori_loop(0, n_chunks, score_chunk, 0)

    def count_ge(thr):
        def body(c, cnt):
            hit = jnp.where(sc_ref[c] >= thr, 1.0, 0.0)
            return cnt + jnp.sum(hit.reshape(tk // 8, 8, tq), axis=0)
        cnt = lax.fori_loop(0, n_chunks, body, jnp.zeros((8, tq), F32))
        return jnp.sum(cnt, 0, keepdims=True)

    def bisect(step, key):
        cand = key + jnp.left_shift(jnp.int32(1), 31 - step)
        ok = count_ge(_key_to_float(cand)) >= float(topk)
        return jnp.where(ok, cand, key)

    key = lax.fori_loop(0, 32, bisect, jnp.full((1, tq), INT_MIN, jnp.int32))
    thr = _key_to_float(jnp.maximum(key, KEY_NEG_FLT_MAX))

    def write_chunk(c, carry):
        o_ref[0, c] = jnp.where(sc_ref[c] >= thr, 0.0, NEG_BIG).astype(BF16)
        return carry

    def fill_chunk(c, carry):
        o_ref[0, c] = jnp.full((tk, tq), NEG_BIG, BF16)
        return carry

    lax.fori_loop(0, n_chunks, write_chunk, 0)
    lax.fori_loop(n_chunks, nk, fill_chunk, 0)


def _dsa_index(proj, wt, kk, batch, seq_len, qi_col_block, tq, tk):
    nq = seq_len // tq
    nk = seq_len // tk
    topk = min(DSA_TOPK_MAX, seq_len // 4)
    qi_w = DSA_IDX_HEADS * DSA_IDX_DIM
    kern = functools.partial(_dsa_index_kernel, tq=tq, tk=tk, topk=topk)
    return pl.pallas_call(
        kern,
        grid=(batch, nq),
        in_specs=[pl.BlockSpec((tq, qi_w), lambda b, i: (b * nq + i, qi_col_block)),
                  pl.BlockSpec((LANES, tq), lambda b, i: (0, b * nq + i)),
                  pl.BlockSpec((seq_len, 2 * LANES), lambda b, i: (b, 0))],
        out_specs=pl.BlockSpec((1, nk, tk, tq), lambda b, i: (b * nq + i, 0, 0, 0)),
        out_shape=jax.ShapeDtypeStruct((batch * nq, nk, tk, tq), BF16),
        scratch_shapes=[pltpu.VMEM((nk, tk, tq), F32)],
        compiler_params=_params("parallel", "parallel"),
        name="dsa_index_topk",
    )(proj, wt, kk)


def _dsa_attn_kernel(q_ref, k_ref, v_ref, bias_ref, o_ref, qs_ref, m_ref, l_ref, acc_ref, s_ref,
                     *, tq, tk):
    i = pl.program_id(1)
    j = pl.program_id(2)
    rep = DSA_HEADS // DSA_KV_HEADS
    dh = k_ref.shape[1] // DSA_KV_HEADS
    last = ((i + 1) * tq - 1) // tk

    @pl.when(j == 0)
    def _():
        m_ref[...] = jnp.full_like(m_ref, NEG_BIG)
        l_ref[...] = jnp.zeros_like(l_ref)
        acc_ref[...] = jnp.zeros_like(acc_ref)
        for hd in range(DSA_HEADS):
            g, r = hd // rep, hd % rep
            qh = q_ref[:, hd * dh:(hd + 1) * dh].astype(F32) * (dh ** -0.5 * LOG2_E)
            qs_ref[g, r * tq:(r + 1) * tq, :] = qh.astype(BF16)

    @pl.when(j <= last)
    def _():
        bias = bias_ref[0, 0].astype(F32)

        def logits(g):
            kg = k_ref[:, g * dh:(g + 1) * dh]
            tops = []
            for r in range(rep):
                cols = slice(r * tq, (r + 1) * tq)
                s = _dot_nt(kg, qs_ref[g, cols, :]) + bias
                s_ref[g, :, cols] = s
                tops.append(jnp.max(s, 0, keepdims=True))
            return jnp.concatenate(tops, axis=1)

        def accumulate(g, smax):
            vg = v_ref[:, g * dh:(g + 1) * dh]
            m_prev = m_ref[g]
            m_new = jnp.maximum(m_prev, smax)
            alpha = jnp.exp2(m_prev - m_new)
            sums, pvs = [], []
            for r in range(rep):
                cols = slice(r * tq, (r + 1) * tq)
                p = jnp.exp2(s_ref[g, :, cols] - m_new[:, cols])
                sums.append(jnp.sum(p, 0, keepdims=True))
                pvs.append(lax.dot_general(vg, p.astype(BF16), (((0,), (0,)), ((), ())),
                                           preferred_element_type=F32))
            l_ref[g] = alpha * l_ref[g] + jnp.concatenate(sums, axis=1)
            acc_ref[g] = alpha * acc_ref[g] + jnp.concatenate(pvs, axis=1)
            m_ref[g] = m_new

        smax = logits(0)
        for g in range(DSA_KV_HEADS):
            nxt = logits(g + 1) if g + 1 < DSA_KV_HEADS else None
            accumulate(g, smax)
            smax = nxt

    @pl.when(j == last)
    def _():
        for g in range(DSA_KV_HEADS):
            og = acc_ref[g] / l_ref[g]
            for r in range(rep):
                hd = g * rep + r
                o_ref[:, hd * dh:(hd + 1) * dh] = og[:, r * tq:(r + 1) * tq].T.astype(BF16)


def _dsa_attn(proj, bias, batch, seq_len, dh, tq, tk):
    m = proj.shape[0]
    nq = seq_len // tq
    nk = seq_len // tk
    dq = DSA_HEADS * dh
    dkv = DSA_KV_HEADS * dh
    rep = DSA_HEADS // DSA_KV_HEADS
    last = lambda i: ((i + 1) * tq - 1) // tk
    kern = functools.partial(_dsa_attn_kernel, tq=tq, tk=tk)
    return pl.pallas_call(
        kern,
        grid=(batch, nq, nk),
        in_specs=[pl.BlockSpec((tq, dq), lambda b, i, j: (b * nq + i, 0)),
                  pl.BlockSpec((tk, dkv), lambda b, i, j: (b * nk + jnp.minimum(j, last(i)), dq // dkv)),
                  pl.BlockSpec((tk, dkv), lambda b, i, j: (b * nk + jnp.minimum(j, last(i)), dq // dkv + 1)),
                  pl.BlockSpec((1, 1, tk, tq),
                               lambda b, i, j: (b * nq + i, jnp.minimum(j, last(i)), 0, 0))],
        out_specs=pl.BlockSpec((tq, dq), lambda b, i, j: (b * nq + i, 0)),
        out_shape=jax.ShapeDtypeStruct((m, dq), BF16),
        scratch_shapes=[pltpu.VMEM((DSA_KV_HEADS, rep * tq, dh), BF16),
                        pltpu.VMEM((DSA_KV_HEADS, 1, rep * tq), F32),
                        pltpu.VMEM((DSA_KV_HEADS, 1, rep * tq), F32),
                        pltpu.VMEM((DSA_KV_HEADS, dh, rep * tq), F32),
                        pltpu.VMEM((DSA_KV_HEADS, tk, rep * tq), F32)],
        compiler_params=_params("parallel", "parallel", "arbitrary"),
        name="dsa_attention",
    )(proj, proj, proj, bias)


def _pad_cols(w, n):
    return jnp.pad(w, ((0, 0), (0, n - w.shape[1])))


def _gla_mixer(xb, batch, seq_len, w_in, w_a1, w_a2, b_a, norm_g):
    qkvr = _matmul(xb, w_in.astype(BF16), BF16, tm=1024, tn=512)
    t = _matmul(xb, _pad_cols(w_a1, LANES).astype(BF16), F32, tm=1024, tn=LANES)
    wa2 = jnp.pad(w_a2, ((0, LANES - w_a2.shape[0]), (0, 0))).astype(BF16)
    return _gla(qkvr, t, wa2, b_a[None, :], norm_g[None, :], batch, seq_len)


def _dsa_mixer(xb, batch, seq_len, w_in, kidx_g, kidx_b):
    d = w_in.shape[0]
    dh = d // DSA_HEADS
    main = d + 2 * DSA_KV_HEADS * dh + DSA_IDX_HEADS * DSA_IDX_DIM
    proj = _matmul(xb, w_in[:, :main].astype(BF16), BF16, tm=1024, tn=512)
    kiwi = _matmul(xb, _pad_cols(w_in[:, main:], LANES).astype(BF16), F32, tm=1024, tn=LANES)
    pad1 = lambda a: jnp.pad(a, (0, LANES - a.shape[0]))[None, :]
    kk, wt = _ki_prep(kiwi, pad1(kidx_g), pad1(kidx_b))
    qi_w = DSA_IDX_HEADS * DSA_IDX_DIM
    tq, tk = 256, 512
    bias = _dsa_index(proj, wt, kk, batch, seq_len, (main - qi_w) // qi_w, tq, tk)
    return _dsa_attn(proj, bias, batch, seq_len, dh, tq, tk)


def kernel(x, p, gla_w_in, gla_w_a1, gla_w_a2, gla_b_a, gla_norm_g, gla_w_o, pool_w, pool_scale,
           dsa_w_in, dsa_kidx_g, dsa_kidx_b, dsa_w_o, ln_mix_g, ln_mix_b, ffn_w_up, ffn_conv_w,
           ffn_conv_b, ffn_w_down, ple_gate_w, ple_proj_w, ln_ffn_g, ln_ffn_b):
    batch, seq_len, d = x.shape
    m = batch * seq_len
    dff = ffn_w_down.shape[1]
    tf = 512
    dffp = -(-dff // tf) * tf
    xf = x.reshape(m, d)
    xb = xf.astype(BF16)
    for i in range(DEPTH):
        kind, j = i % N_MIXERS, i // N_MIXERS
        g_mix, b_mix = ln_mix_g[i][None, :], ln_mix_b[i][None, :]
        if kind == 0:
            o = _gla_mixer(xb, batch, seq_len, gla_w_in[j], gla_w_a1[j], gla_w_a2[j], gla_b_a[j],
                           gla_norm_g[j])
            xf, xb = _proj_res_ln(o, gla_w_o[j].astype(BF16), xf, g_mix, b_mix)
        elif kind == 1:
            xf, xb = _pool(xf, pool_w[j].astype(BF16), pool_scale[j][None, :], g_mix, b_mix, seq_len)
        else:
            o = _dsa_mixer(xb, batch, seq_len, dsa_w_in[j], dsa_kidx_g[j], dsa_kidx_b[j])
            xf, xb = _proj_res_ln(o, dsa_w_o[j].astype(BF16), xf, g_mix, b_mix)

        w_up = ffn_w_up[i]
        wg = _pad_cols(w_up[:, :dff], dffp).astype(BF16)
        wu = _pad_cols(w_up[:, dff:], dffp).astype(BF16)
        cw, cb = ffn_conv_w[i], ffn_conv_b[i]
        rows_g = jnp.concatenate([cw[:, :dff], cb[None, :dff]], 0)
        rows_u = jnp.concatenate([cw[:, dff:], cb[None, dff:]], 0)
        convp = jnp.concatenate([_pad_cols(rows_g, dffp), _pad_cols(rows_u, dffp)], 1)
        convp = jnp.pad(convp, ((0, 8 - convp.shape[0]), (0, 0)))
        wd = jnp.pad(ffn_w_down[i], ((0, dffp - dff), (0, 0))).astype(BF16)
        f = _ffn(xb, wg, wu, convp, wd, seq_len, tf=tf)
        xf, xb = _ple_ln(xb, xf, f, p[i].reshape(m, -1), ple_gate_w[i].astype(BF16),
                         ple_proj_w[i].astype(BF16), ln_ffn_g[i][None, :], ln_ffn_b[i][None, :])
    return xf.reshape(batch, seq_len, d)
```

```python
import functools

import jax
import jax.numpy as jnp
from jax import lax
from jax.experimental import pallas as pl
from jax.experimental.pallas import tpu as pltpu

F32 = jnp.float32
BF16 = jnp.bfloat16

DEPTH = 4
N_MIXERS = 3
LN_EPS = 1e-5
ALPHA = (2.0 * DEPTH) ** 0.25
GLA_HEADS = 4
GLA_TAU = 16.0
POOL_WINDOWS = (2, 4, 8, 16)
DSA_HEADS = 16
DSA_KV_HEADS = 4
DSA_IDX_HEADS = 16
DSA_IDX_DIM = 64
DSA_TOPK_MAX = 256
CONV_WIDTH = 3

LANES = 128
BF16_SUBLANES = 16
VMEM_LIMIT_BYTES = 56 * 1024 * 1024

NEG_BIG = -1e30
LOG2_E = 1.4426950408889634
INT_MIN = -(2 ** 31)
KEY_NEG_FLT_MAX = INT_MIN + 0x800000


def _params(*sem):
    return pltpu.CompilerParams(dimension_semantics=sem, vmem_limit_bytes=VMEM_LIMIT_BYTES)


def _dot(a, b):
    return jnp.dot(a, b, preferred_element_type=F32)


def _dot_nt(a, b):
    return lax.dot_general(a, b, (((1,), (1,)), ((), ())), preferred_element_type=F32)


def _layer_norm(y, g, b):
    mu = jnp.mean(y, -1, keepdims=True)
    yc = y - mu
    var = jnp.mean(yc * yc, -1, keepdims=True)
    return yc * lax.rsqrt(var + LN_EPS) * g + b


def _matmul_kernel(x_ref, w_ref, o_ref):
    o_ref[...] = _dot(x_ref[...], w_ref[...]).astype(o_ref.dtype)


def _matmul(x, w, out_dtype, tm, tn):
    m, k = x.shape
    n = w.shape[1]
    return pl.pallas_call(
        _matmul_kernel,
        grid=(m // tm, n // tn),
        in_specs=[pl.BlockSpec((tm, k), lambda i, j: (i, 0)),
                  pl.BlockSpec((k, tn), lambda i, j: (0, j))],
        out_specs=pl.BlockSpec((tm, tn), lambda i, j: (i, j)),
        out_shape=jax.ShapeDtypeStruct((m, n), out_dtype),
        compiler_params=_params("parallel", "parallel"),
        name="matmul",
    )(x, w)


def _proj_res_ln_kernel(a_ref, w_ref, x_ref, g_ref, b_ref, of_ref, ob_ref):
    m = _dot(a_ref[...], w_ref[...])
    y = _layer_norm(ALPHA * x_ref[...] + m, g_ref[...], b_ref[...])
    of_ref[...] = y
    ob_ref[...] = y.astype(BF16)


def _proj_res_ln(a, w, x, g, b, tm=256):
    m, ka = a.shape
    d = w.shape[1]
    row = lambda i: (i, 0)
    const = lambda i: (0, 0)
    return pl.pallas_call(
        _proj_res_ln_kernel,
        grid=(m // tm,),
        in_specs=[pl.BlockSpec((tm, ka), row), pl.BlockSpec((ka, d), const),
                  pl.BlockSpec((tm, d), row), pl.BlockSpec((1, d), const),
                  pl.BlockSpec((1, d), const)],
        out_specs=[pl.BlockSpec((tm, d), row), pl.BlockSpec((tm, d), row)],
        out_shape=[jax.ShapeDtypeStruct((m, d), F32), jax.ShapeDtypeStruct((m, d), BF16)],
        compiler_params=_params("parallel"),
        name="proj_res_ln",
    )(a, w, x, g, b)


def _ffn_kernel(x_ref, halo_ref, wg_ref, wu_ref, cg_ref, cu_ref, wd_ref, o_ref,
                xs_ref, hg_ref, hu_ref, *, tm, blocks_per_seq):
    i = pl.program_id(0)
    j = pl.program_id(1)
    hr = BF16_SUBLANES

    @pl.when(j == 0)
    def _():
        first = (i % blocks_per_seq) == 0
        halo = halo_ref[...]
        xs_ref[0:hr, :] = jnp.where(first, jnp.zeros_like(halo), halo)
        xs_ref[hr:, :] = x_ref[...]
        o_ref[...] = jnp.zeros_like(o_ref)

    xs = xs_ref[...]
    hg_ref[...] = _dot(xs, wg_ref[...])
    hu_ref[...] = _dot(xs, wu_ref[...])

    def conv(h_ref, c_ref):
        c = c_ref[...]
        out = c[CONV_WIDTH:CONV_WIDTH + 1]
        for tap in range(CONV_WIDTH):
            shift = CONV_WIDTH - 1 - tap
            out = out + h_ref[pl.ds(hr - shift, tm), :] * c[tap:tap + 1]
        return out

    g = conv(hg_ref, cg_ref)
    u = conv(hu_ref, cu_ref)
    gelu = 0.5 * g * (1.0 + lax.erf(g * (0.5 ** 0.5)))
    act = (gelu * u).astype(BF16)
    o_ref[...] += _dot(act, wd_ref[...])


def _ffn(xb, wg, wu, convp, wd, seq_len, tm=512, tf=512):
    m, d = xb.shape
    dffp = wg.shape[1]
    nf = dffp // tf
    hr = BF16_SUBLANES
    kern = functools.partial(_ffn_kernel, tm=tm, blocks_per_seq=seq_len // tm)
    return pl.pallas_call(
        kern,
        grid=(m // tm, nf),
        in_specs=[pl.BlockSpec((tm, d), lambda i, j: (i, 0)),
                  pl.BlockSpec((hr, d), lambda i, j: (jnp.maximum(i * (tm // hr) - 1, 0), 0)),
                  pl.BlockSpec((d, tf), lambda i, j: (0, j)),
                  pl.BlockSpec((d, tf), lambda i, j: (0, j)),
                  pl.BlockSpec((8, tf), lambda i, j: (0, j)),
                  pl.BlockSpec((8, tf), lambda i, j: (0, nf + j)),
                  pl.BlockSpec((tf, d), lambda i, j: (j, 0))],
        out_specs=pl.BlockSpec((tm, d), lambda i, j: (i, 0)),
        out_shape=jax.ShapeDtypeStruct((m, d), F32),
        scratch_shapes=[pltpu.VMEM((tm + hr, d), BF16),
                        pltpu.VMEM((tm + hr, tf), F32),
                        pltpu.VMEM((tm + hr, tf), F32)],
        compiler_params=_params("parallel", "arbitrary"),
        name="conv_ffn",
    )(xb, xb, wg, wu, convp, convp, wd)


def _ple_ln_kernel(xb_ref, x_ref, f_ref, p_ref, wg_ref, wp_ref, g_ref, b_ref, of_ref, ob_ref):
    gate = jax.nn.sigmoid(_dot(xb_ref[...], wg_ref[...]))
    proj = _dot(p_ref[...].astype(BF16), wp_ref[...])
    y = _layer_norm(ALPHA * x_ref[...] + f_ref[...] + gate * proj, g_ref[...], b_ref[...])
    of_ref[...] = y
    ob_ref[...] = y.astype(BF16)


def _ple_ln(xb, x, f, p, wg, wp, g, b, tm=256):
    m, d = x.shape
    pd = p.shape[1]
    row = lambda i: (i, 0)
    const = lambda i: (0, 0)
    return pl.pallas_call(
        _ple_ln_kernel,
        grid=(m // tm,),
        in_specs=[pl.BlockSpec((tm, d), row), pl.BlockSpec((tm, d), row),
                  pl.BlockSpec((tm, d), row), pl.BlockSpec((tm, pd), row),
                  pl.BlockSpec((d, d), const), pl.BlockSpec((pd, d), const),
                  pl.BlockSpec((1, d), const), pl.BlockSpec((1, d), const)],
        out_specs=[pl.BlockSpec((tm, d), row), pl.BlockSpec((tm, d), row)],
        out_shape=[jax.ShapeDtypeStruct((m, d), F32), jax.ShapeDtypeStruct((m, d), BF16)],
        compiler_params=_params("parallel"),
        name="ple_ln",
    )(xb, x, f, p, wg, wp, g, b)


def _split3(a):
    hi = a.astype(BF16)
    r1 = a - hi.astype(F32)
    mid = r1.astype(BF16)
    lo = (r1 - mid.astype(F32)).astype(BF16)
    return hi, mid, lo


def _gla_kernel(q_ref, k_ref, v_ref, r_ref, t_ref, wa2_ref, ba_ref, ng_ref, o_ref, s_ref, *, chunk):
    c = pl.program_id(2)
    dk = q_ref.shape[1]

    @pl.when(c == 0)
    def _():
        s_ref[...] = jnp.zeros_like(s_ref)

    z = _dot(t_ref[...].astype(BF16), wa2_ref[...]) + ba_ref[...]
    la = (jnp.minimum(z, 0.0) - jnp.log1p(jnp.exp(-jnp.abs(z)))) * (1.0 / GLA_TAU)
    row = lax.broadcasted_iota(jnp.int32, (chunk, chunk), 0)
    col = lax.broadcasted_iota(jnp.int32, (chunk, chunk), 1)
    causal = row >= col
    tri = jnp.where(causal, 1.0, 0.0).astype(BF16)
    hi, mid, lo = _split3(la)
    b = _dot(tri, hi) + _dot(tri, mid) + _dot(tri, lo)

    eb = jnp.exp(b)
    qt = (q_ref[...].astype(F32) * (dk ** -0.5) * eb).astype(BF16)
    kt = k_ref[...].astype(F32) * jnp.exp(-b)
    v = v_ref[...]
    state = s_ref[...]

    o = _dot(qt, state.astype(BF16))
    scores = jnp.where(causal, _dot_nt(qt, kt.astype(BF16)), 0.0).astype(BF16)
    o = o + _dot(scores, v)

    eb_last = eb[chunk - LANES:, :].T[:, LANES - 1:LANES]
    s_ref[...] = (state + _dot(kt.T.astype(BF16), v)) * eb_last

    o = o * lax.rsqrt(jnp.mean(o * o, -1, keepdims=True) + LN_EPS) * ng_ref[...]
    r = r_ref[...].astype(F32)
    o_ref[...] = (o * (r * jax.nn.sigmoid(r))).astype(BF16)


def _gla(qkvr, t, wa2, ba, ng, batch, seq_len, chunk=256):
    m = qkvr.shape[0]
    h = GLA_HEADS
    dk = wa2.shape[1] // h
    dv = ng.shape[1] // h
    nc = seq_len // chunk
    rows = lambda b, hh, c: b * nc + c
    kern = functools.partial(_gla_kernel, chunk=chunk)
    return pl.pallas_call(
        kern,
        grid=(batch, h, nc),
        in_specs=[pl.BlockSpec((chunk, dk), lambda b, hh, c: (rows(b, hh, c), hh)),
                  pl.BlockSpec((chunk, dk), lambda b, hh, c: (rows(b, hh, c), h + hh)),
                  pl.BlockSpec((chunk, dv), lambda b, hh, c: (rows(b, hh, c), (2 * h * dk) // dv + hh)),
                  pl.BlockSpec((chunk, dv), lambda b, hh, c: (rows(b, hh, c), (2 * h * dk) // dv + h + hh)),
                  pl.BlockSpec((chunk, LANES), lambda b, hh, c: (rows(b, hh, c), 0)),
                  pl.BlockSpec((LANES, dk), lambda b, hh, c: (0, hh)),
                  pl.BlockSpec((1, dk), lambda b, hh, c: (0, hh)),
                  pl.BlockSpec((1, dv), lambda b, hh, c: (0, hh))],
        out_specs=pl.BlockSpec((chunk, dv), lambda b, hh, c: (rows(b, hh, c), hh)),
        out_shape=jax.ShapeDtypeStruct((m, h * dv), BF16),
        scratch_shapes=[pltpu.VMEM((dk, dv), F32)],
        compiler_params=_params("parallel", "parallel", "arbitrary"),
        name="gla_chunk",
    )(qkvr, qkvr, qkvr, qkvr, t, wa2, ba, ng)


def _pool_kernel(x_ref, halo_ref, w_ref, sc_ref, g_ref, b_ref, of_ref, ob_ref, xe_ref,
                 *, tm, blocks_per_seq, halo_rows):
    i = pl.program_id(0)
    blk = i % blocks_per_seq
    halo = halo_ref[...]
    xe_ref[0:halo_rows, :] = jnp.where(blk == 0, jnp.zeros_like(halo), halo)
    xe_ref[halo_rows:, :] = x_ref[...]
    gw = w_ref.shape[1]
    pos = blk * tm + lax.broadcasted_iota(jnp.int32, (tm, 1), 0)
    ys = []
    for g, w in enumerate(POOL_WINDOWS):
        cols = slice(g * gw, (g + 1) * gw)
        xg = xe_ref[pl.ds(halo_rows, tm), cols]
        acc = xg
        for dlt in range(1, w):
            acc = acc + xe_ref[pl.ds(halo_rows - dlt, tm), cols]
        cnt = jnp.minimum(pos + 1, w).astype(F32)
        pooled = acc / cnt - xg
        ys.append(_dot(pooled.astype(BF16), w_ref[g]))
    y = jnp.concatenate(ys, axis=-1) * sc_ref[...]
    out = _layer_norm(ALPHA * x_ref[...] + y, g_ref[...], b_ref[...])
    of_ref[...] = out
    ob_ref[...] = out.astype(BF16)


def _pool(x, w, scale, g, b, seq_len, tm=256):
    m, d = x.shape
    halo_rows = max(POOL_WINDOWS)
    kern = functools.partial(_pool_kernel, tm=tm, blocks_per_seq=seq_len // tm, halo_rows=halo_rows)
    row = lambda i: (i, 0)
    const = lambda i: (0, 0)
    return pl.pallas_call(
        kern,
        grid=(m // tm,),
        in_specs=[pl.BlockSpec((tm, d), row),
                  pl.BlockSpec((halo_rows, d), lambda i: (jnp.maximum(i * (tm // halo_rows) - 1, 0), 0)),
                  pl.BlockSpec(w.shape, lambda i: (0, 0, 0)),
                  pl.BlockSpec((1, d), const), pl.BlockSpec((1, d), const), pl.BlockSpec((1, d), const)],
        out_specs=[pl.BlockSpec((tm, d), row), pl.BlockSpec((tm, d), row)],
        out_shape=[jax.ShapeDtypeStruct((m, d), F32), jax.ShapeDtypeStruct((m, d), BF16)],
        scratch_shapes=[pltpu.VMEM((tm + halo_rows, d), F32)],
        compiler_params=_params("parallel"),
        name="pool_mixer",
    )(x, x, w, scale, g, b)


def _ki_prep_kernel(kw_ref, g_ref, b_ref, o_ref, wt_ref):
    x = kw_ref[...]
    wt_ref[...] = x.T * ((DSA_IDX_HEADS ** -0.5) * (DSA_IDX_DIM ** -0.5))
    lane = lax.broadcasted_iota(jnp.int32, x.shape, 1)
    isk = lane < DSA_IDX_DIM
    mu = jnp.sum(jnp.where(isk, x, 0.0), -1, keepdims=True) * (1.0 / DSA_IDX_DIM)
    xc = jnp.where(isk, x - mu, 0.0)
    var = jnp.sum(xc * xc, -1, keepdims=True) * (1.0 / DSA_IDX_DIM)
    kn = jnp.where(isk, xc * lax.rsqrt(var + LN_EPS) * g_ref[...] + b_ref[...], 0.0)
    o_ref[:, 0:LANES] = kn.astype(BF16)
    o_ref[:, LANES:2 * LANES] = pltpu.roll(kn, DSA_IDX_DIM, axis=1).astype(BF16)


def _ki_prep(kiwi, g, b, tm=1024):
    m = kiwi.shape[0]
    return pl.pallas_call(
        _ki_prep_kernel,
        grid=(m // tm,),
        in_specs=[pl.BlockSpec((tm, LANES), lambda i: (i, 0)),
                  pl.BlockSpec((1, LANES), lambda i: (0, 0)),
                  pl.BlockSpec((1, LANES), lambda i: (0, 0))],
        out_specs=[pl.BlockSpec((tm, 2 * LANES), lambda i: (i, 0)),
                   pl.BlockSpec((LANES, tm), lambda i: (0, i))],
        out_shape=[jax.ShapeDtypeStruct((m, 2 * LANES), BF16),
                   jax.ShapeDtypeStruct((LANES, m), F32)],
        compiler_params=_params("parallel"),
        name="dsa_ki_prep",
    )(kiwi, g, b)


def _key_to_float(key):
    bits = jnp.where(key < 0, key ^ 0x7FFFFFFF, key)
    return lax.bitcast_convert_type(bits, F32)


def _dsa_index_kernel(qi_ref, wt_ref, kk_ref, o_ref, sc_ref, *, tq, tk, topk):
    i = pl.program_id(1)
    nk = sc_ref.shape[0]
    t0 = i * tq
    n_chunks = (t0 + tq + tk - 1) // tk
    spos = lax.broadcasted_iota(jnp.int32, (tk, tq), 0)
    tpos = t0 + lax.broadcasted_iota(jnp.int32, (tk, tq), 1)

    def score_chunk(c, carry):
        start = pl.multiple_of(c * tk, tk)
        ka = kk_ref[pl.ds(start, tk), 0:LANES]
        kb = kk_ref[pl.ds(start, tk), LANES:2 * LANES]
        acc = jnp.zeros((tk, tq), F32)
        for pair in range(DSA_IDX_HEADS // 2):
            qp = qi_ref[:, pair * LANES:(pair + 1) * LANES]
            row = DSA_IDX_DIM + 2 * pair
            acc = acc + jnp.maximum(_dot_nt(ka, qp), 0.0) * wt_ref[row:row + 1, :]
            acc = acc + jnp.maximum(_dot_nt(kb, qp), 0.0) * wt_ref[row + 1:row + 2, :]
        sc_ref[c] = jnp.where(c * tk + spos <= tpos, acc, -jnp.inf)
        return carry

    lax.fori_loop(0, n_chunks, score_chunk, 0)

    acc_rows = 32

    def count_ge(thr):
        def body(c, cnt):
            hit = jnp.where(sc_ref[c] >= thr, 1.0, 0.0)
            return cnt + jnp.sum(hit.reshape(tk // acc_rows, acc_rows, tq), axis=0)
        cnt = lax.fori_loop(0, n_chunks, body, jnp.zeros((acc_rows, tq), F32))
        return jnp.sum(cnt, 0, keepdims=True)

    def bisect(step, key):
        cand = key + jnp.left_shift(jnp.int32(1), 31 - step)
        ok = count_ge(_key_to_float(cand)) >= float(topk)
        return jnp.where(ok, cand, key)

    key = lax.fori_loop(0, 32, bisect, jnp.full((1, tq), INT_MIN, jnp.int32))
    thr = _key_to_float(jnp.maximum(key, KEY_NEG_FLT_MAX))

    def write_chunk(c, carry):
        o_ref[0, c] = jnp.where(sc_ref[c] >= thr, 0.0, NEG_BIG).astype(BF16)
        return carry

    def fill_chunk(c, carry):
        o_ref[0, c] = jnp.full((tk, tq), NEG_BIG, BF16)
        return carry

    lax.fori_loop(0, n_chunks, write_chunk, 0)
    lax.fori_loop(n_chunks, nk, fill_chunk, 0)


def _dsa_index(proj, wt, kk, batch, seq_len, qi_col_block, tq, tk):
    nq = seq_len // tq
    nk = seq_len // tk
    topk = min(DSA_TOPK_MAX, seq_len // 4)
    qi_w = DSA_IDX_HEADS * DSA_IDX_DIM
    kern = functools.partial(_dsa_index_kernel, tq=tq, tk=tk, topk=topk)
    return pl.pallas_call(
        kern,
        grid=(batch, nq),
        in_specs=[pl.BlockSpec((tq, qi_w), lambda b, i: (b * nq + i, qi_col_block)),
                  pl.BlockSpec((LANES, tq), lambda b, i: (0, b * nq + i)),
                  pl.BlockSpec((seq_len, 2 * LANES), lambda b, i: (b, 0))],
        out_specs=pl.BlockSpec((1, nk, tk, tq), lambda b, i: (b * nq + i, 0, 0, 0)),
        out_shape=jax.ShapeDtypeStruct((batch * nq, nk, tk, tq), BF16),
        scratch_shapes=[pltpu.VMEM((nk, tk, tq), F32)],
        compiler_params=_params("parallel", "parallel"),
        name="dsa_index_topk",
    )(proj, wt, kk)


def _dsa_attn_kernel(qblk_ref, kblk_ref, q_ref, k_ref, v_ref, bias_ref, o_ref,
                     qs_ref, m_ref, l_ref, acc_ref, s_ref, *, tq, tk):
    step = pl.program_id(1)
    i = qblk_ref[step]
    j = kblk_ref[step]
    rep = DSA_HEADS // DSA_KV_HEADS
    dh = k_ref.shape[1] // DSA_KV_HEADS
    last = ((i + 1) * tq - 1) // tk

    @pl.when(j == 0)
    def _():
        m_ref[...] = jnp.full_like(m_ref, NEG_BIG)
        l_ref[...] = jnp.zeros_like(l_ref)
        acc_ref[...] = jnp.zeros_like(acc_ref)
        for hd in range(DSA_HEADS):
            g, r = hd // rep, hd % rep
            qh = q_ref[:, hd * dh:(hd + 1) * dh].astype(F32) * (dh ** -0.5 * LOG2_E)
            qs_ref[g, r * tq:(r + 1) * tq, :] = qh.astype(BF16)

    bias = bias_ref[0, 0].astype(F32)

    def logits(g):
        kg = k_ref[:, g * dh:(g + 1) * dh]
        tops = []
        for r in range(rep):
            cols = slice(r * tq, (r + 1) * tq)
            s = _dot_nt(kg, qs_ref[g, cols, :]) + bias
            s_ref[g, :, cols] = s
            tops.append(jnp.max(s, 0, keepdims=True))
        return jnp.concatenate(tops, axis=1)

    def accumulate(g, smax):
        vg = v_ref[:, g * dh:(g + 1) * dh]
        m_prev = m_ref[g]
        m_new = jnp.maximum(m_prev, smax)
        alpha = jnp.exp2(m_prev - m_new)
        sums, pvs = [], []
        for r in range(rep):
            cols = slice(r * tq, (r + 1) * tq)
            p = jnp.exp2(s_ref[g, :, cols] - m_new[:, cols])
            sums.append(jnp.sum(p, 0, keepdims=True))
            pvs.append(lax.dot_general(vg, p.astype(BF16), (((0,), (0,)), ((), ())),
                                       preferred_element_type=F32))
        l_ref[g] = alpha * l_ref[g] + jnp.concatenate(sums, axis=1)
        acc_ref[g] = alpha * acc_ref[g] + jnp.concatenate(pvs, axis=1)
        m_ref[g] = m_new

    smax = logits(0)
    for g in range(DSA_KV_HEADS):
        nxt = logits(g + 1) if g + 1 < DSA_KV_HEADS else None
        accumulate(g, smax)
        smax = nxt

    @pl.when(j == last)
    def _():
        for g in range(DSA_KV_HEADS):
            og = acc_ref[g] / l_ref[g]
            for r in range(rep):
                hd = g * rep + r
                o_ref[:, hd * dh:(hd + 1) * dh] = og[:, r * tq:(r + 1) * tq].T.astype(BF16)


def _dsa_attn(proj, bias, batch, seq_len, dh, tq, tk):
    m = proj.shape[0]
    nq = seq_len // tq
    nk = seq_len // tk
    dq = DSA_HEADS * dh
    dkv = DSA_KV_HEADS * dh
    rep = DSA_HEADS // DSA_KV_HEADS
    pairs = [(i, j) for i in range(nq) for j in range(((i + 1) * tq - 1) // tk + 1)]
    qblk = jnp.asarray([pr[0] for pr in pairs], jnp.int32)
    kblk = jnp.asarray([pr[1] for pr in pairs], jnp.int32)
    kern = functools.partial(_dsa_attn_kernel, tq=tq, tk=tk)
    grid_spec = pltpu.PrefetchScalarGridSpec(
        num_scalar_prefetch=2,
        grid=(batch, len(pairs)),
        in_specs=[pl.BlockSpec((tq, dq), lambda b, s, qb, kb: (b * nq + qb[s], 0)),
                  pl.BlockSpec((tk, dkv), lambda b, s, qb, kb: (b * nk + kb[s], dq // dkv)),
                  pl.BlockSpec((tk, dkv), lambda b, s, qb, kb: (b * nk + kb[s], dq // dkv + 1)),
                  pl.BlockSpec((1, 1, tk, tq), lambda b, s, qb, kb: (b * nq + qb[s], kb[s], 0, 0))],
        out_specs=pl.BlockSpec((tq, dq), lambda b, s, qb, kb: (b * nq + qb[s], 0)),
        scratch_shapes=[pltpu.VMEM((DSA_KV_HEADS, rep * tq, dh), BF16),
                        pltpu.VMEM((DSA_KV_HEADS, 1, rep * tq), F32),
                        pltpu.VMEM((DSA_KV_HEADS, 1, rep * tq), F32),
                        pltpu.VMEM((DSA_KV_HEADS, dh, rep * tq), F32),
                        pltpu.VMEM((DSA_KV_HEADS, tk, rep * tq), F32)])
    return pl.pallas_call(
        kern,
        grid_spec=grid_spec,
        out_shape=jax.ShapeDtypeStruct((m, dq), BF16),
        compiler_params=_params("parallel", "arbitrary"),
        name="dsa_attention",
    )(qblk, kblk, proj, proj, proj, bias)


def _pad_cols(w, n):
    return jnp.pad(w, ((0, 0), (0, n - w.shape[1])))


def _gla_mixer(xb, batch, seq_len, w_in, w_a1, w_a2, b_a, norm_g):
    qkvr = _matmul(xb, w_in.astype(BF16), BF16, tm=1024, tn=512)
    t = _matmul(xb, _pad_cols(w_a1, LANES).astype(BF16), F32, tm=1024, tn=LANES)
    wa2 = jnp.pad(w_a2, ((0, LANES - w_a2.shape[0]), (0, 0))).astype(BF16)
    return _gla(qkvr, t, wa2, b_a[None, :], norm_g[None, :], batch, seq_len)


def _dsa_mixer(xb, batch, seq_len, w_in, kidx_g, kidx_b):
    d = w_in.shape[0]
    dh = d // DSA_HEADS
    main = d + 2 * DSA_KV_HEADS * dh + DSA_IDX_HEADS * DSA_IDX_DIM
    proj = _matmul(xb, w_in[:, :main].astype(BF16), BF16, tm=1024, tn=512)
    kiwi = _matmul(xb, _pad_cols(w_in[:, main:], LANES).astype(BF16), F32, tm=1024, tn=LANES)
    pad1 = lambda a: jnp.pad(a, (0, LANES - a.shape[0]))[None, :]
    kk, wt = _ki_prep(kiwi, pad1(kidx_g), pad1(kidx_b))
    qi_w = DSA_IDX_HEADS * DSA_IDX_DIM
    tq, tk = 256, 512
    bias = _dsa_index(proj, wt, kk, batch, seq_len, (main - qi_w) // qi_w, tq, tk)
    return _dsa_attn(proj, bias, batch, seq_len, dh, tq, tk)


def kernel(x, p, gla_w_in, gla_w_a1, gla_w_a2, gla_b_a, gla_norm_g, gla_w_o, pool_w, pool_scale,
           dsa_w_in, dsa_kidx_g, dsa_kidx_b, dsa_w_o, ln_mix_g, ln_mix_b, ffn_w_up, ffn_conv_w,
           ffn_conv_b, ffn_w_down, ple_gate_w, ple_proj_w, ln_ffn_g, ln_ffn_b):
    batch, seq_len, d = x.shape
    m = batch * seq_len
    dff = ffn_w_down.shape[1]
    tf = 512
    dffp = -(-dff // tf) * tf
    xf = x.reshape(m, d)
    xb = xf.astype(BF16)
    w_up_b, w_down_b = ffn_w_up.astype(BF16), ffn_w_down.astype(BF16)
    gate_b, proj_b = ple_gate_w.astype(BF16), ple_proj_w.astype(BF16)
    for i in range(DEPTH):
        kind, j = i % N_MIXERS, i // N_MIXERS
        g_mix, b_mix = ln_mix_g[i][None, :], ln_mix_b[i][None, :]
        if kind == 0:
            o = _gla_mixer(xb, batch, seq_len, gla_w_in[j], gla_w_a1[j], gla_w_a2[j], gla_b_a[j],
                           gla_norm_g[j])
            xf, xb = _proj_res_ln(o, gla_w_o[j].astype(BF16), xf, g_mix, b_mix)
        elif kind == 1:
            xf, xb = _pool(xf, pool_w[j].astype(BF16), pool_scale[j][None, :], g_mix, b_mix, seq_len)
        else:
            o = _dsa_mixer(xb, batch, seq_len, dsa_w_in[j], dsa_kidx_g[j], dsa_kidx_b[j])
            xf, xb = _proj_res_ln(o, dsa_w_o[j].astype(BF16), xf, g_mix, b_mix)

        w_up = w_up_b[i]
        wg = _pad_cols(w_up[:, :dff], dffp)
        wu = _pad_cols(w_up[:, dff:], dffp)
        cw, cb = ffn_conv_w[i], ffn_conv_b[i]
        rows_g = jnp.concatenate([cw[:, :dff], cb[None, :dff]], 0)
        rows_u = jnp.concatenate([cw[:, dff:], cb[None, dff:]], 0)
        convp = jnp.concatenate([_pad_cols(rows_g, dffp), _pad_cols(rows_u, dffp)], 1)
        convp = jnp.pad(convp, ((0, 8 - convp.shape[0]), (0, 0)))
        wd = jnp.pad(w_down_b[i], ((0, dffp - dff), (0, 0)))
        f = _ffn(xb, wg, wu, convp, wd, seq_len, tf=tf)
        xf, xb = _ple_ln(xb, xf, f, p[i].reshape(m, -1), gate_b[i], proj_b[i],
                         ln_ffn_g[i][None, :], ln_ffn_b[i][None, :])
    return xf.reshape(batch, seq_len, d)
```

```python
import functools

import jax
import jax.numpy as jnp
from jax import lax
from jax.experimental import pallas as pl
from jax.experimental.pallas import tpu as pltpu

F32 = jnp.float32
BF16 = jnp.bfloat16

DEPTH = 4
N_MIXERS = 3
LN_EPS = 1e-5
ALPHA = (2.0 * DEPTH) ** 0.25
GLA_HEADS = 4
GLA_TAU = 16.0
GLA_SUBBLOCKS = 4
GLA_EXP_CLAMP = 80.0
POOL_WINDOWS = (2, 4, 8, 16)
DSA_HEADS = 16
DSA_KV_HEADS = 4
DSA_IDX_HEADS = 16
DSA_IDX_DIM = 64
DSA_TOPK_MAX = 256
CONV_WIDTH = 3

LANES = 128
BF16_SUBLANES = 16
VMEM_LIMIT_BYTES = 56 * 1024 * 1024

NEG_BIG = -(2.0 ** 100)
LOG2_E = 1.4426950408889634
INT_MIN = -(2 ** 31)
KEY_NEG_FLT_MAX = INT_MIN + 0x800000


def _params(*sem):
    return pltpu.CompilerParams(dimension_semantics=sem, vmem_limit_bytes=VMEM_LIMIT_BYTES)


def _dot(a, b):
    return jnp.dot(a, b, preferred_element_type=F32)


def _dot_nt(a, b):
    return lax.dot_general(a, b, (((1,), (1,)), ((), ())), preferred_element_type=F32)


def _layer_norm(y, g, b):
    mu = jnp.mean(y, -1, keepdims=True)
    yc = y - mu
    var = jnp.mean(yc * yc, -1, keepdims=True)
    return yc * lax.rsqrt(var + LN_EPS) * g + b


def _matmul_kernel(x_ref, w_ref, o_ref):
    o_ref[...] = _dot(x_ref[...], w_ref[...]).astype(o_ref.dtype)


def _matmul(x, w, out_dtype, tm, tn):
    m, k = x.shape
    n = w.shape[1]
    return pl.pallas_call(
        _matmul_kernel,
        grid=(m // tm, n // tn),
        in_specs=[pl.BlockSpec((tm, k), lambda i, j: (i, 0)),
                  pl.BlockSpec((k, tn), lambda i, j: (0, j))],
        out_specs=pl.BlockSpec((tm, tn), lambda i, j: (i, j)),
        out_shape=jax.ShapeDtypeStruct((m, n), out_dtype),
        compiler_params=_params("parallel", "parallel"),
        name="matmul",
    )(x, w)


def _proj_res_ln_kernel(a_ref, w_ref, x_ref, g_ref, b_ref, of_ref, ob_ref):
    m = _dot(a_ref[...], w_ref[...])
    y = _layer_norm(ALPHA * x_ref[...] + m, g_ref[...], b_ref[...])
    of_ref[...] = y
    ob_ref[...] = y.astype(BF16)


def _proj_res_ln(a, w, x, g, b, tm=256):
    m, ka = a.shape
    d = w.shape[1]
    row = lambda i: (i, 0)
    const = lambda i: (0, 0)
    return pl.pallas_call(
        _proj_res_ln_kernel,
        grid=(m // tm,),
        in_specs=[pl.BlockSpec((tm, ka), row), pl.BlockSpec((ka, d), const),
                  pl.BlockSpec((tm, d), row), pl.BlockSpec((1, d), const),
                  pl.BlockSpec((1, d), const)],
        out_specs=[pl.BlockSpec((tm, d), row), pl.BlockSpec((tm, d), row)],
        out_shape=[jax.ShapeDtypeStruct((m, d), F32), jax.ShapeDtypeStruct((m, d), BF16)],
        compiler_params=_params("parallel"),
        name="proj_res_ln",
    )(a, w, x, g, b)


def _ffn_kernel(x_ref, halo_ref, wg_ref, wu_ref, cg_ref, cu_ref, wd_ref, o_ref,
                xs_ref, hg_ref, hu_ref, *, tm, blocks_per_seq):
    i = pl.program_id(0)
    j = pl.program_id(1)
    hr = BF16_SUBLANES

    @pl.when(j == 0)
    def _():
        first = (i % blocks_per_seq) == 0
        halo = halo_ref[...]
        xs_ref[0:hr, :] = jnp.where(first, jnp.zeros_like(halo), halo)
        xs_ref[hr:, :] = x_ref[...]
        o_ref[...] = jnp.zeros_like(o_ref)

    xs = xs_ref[...]
    hg_ref[...] = _dot(xs, wg_ref[...])
    hu_ref[...] = _dot(xs, wu_ref[...])

    def conv(h_ref, c_ref):
        c = c_ref[...]
        out = c[CONV_WIDTH:CONV_WIDTH + 1]
        for tap in range(CONV_WIDTH):
            shift = CONV_WIDTH - 1 - tap
            out = out + h_ref[pl.ds(hr - shift, tm), :] * c[tap:tap + 1]
        return out

    g = conv(hg_ref, cg_ref)
    u = conv(hu_ref, cu_ref)
    gelu = 0.5 * g * (1.0 + lax.erf(g * (0.5 ** 0.5)))
    act = (gelu * u).astype(BF16)
    o_ref[...] += _dot(act, wd_ref[...])


def _ffn_block_starts(dff, tf):
    nf = -(-dff // tf)
    return [min(j * tf, dff - tf) for j in range(nf)]


def _ffn_conv_params(conv_w, conv_b, dff, tf):
    starts = _ffn_block_starts(dff, tf)
    col = jnp.asarray([[s + c for c in range(tf)] for s in starts], jnp.int32)
    fresh = col >= jnp.asarray([[j * tf] for j in range(len(starts))], jnp.int32)
    rows = jnp.concatenate([conv_w, conv_b[None, :]], 0)
    halves = []
    for off in (0, dff):
        part = jnp.where(fresh[None], rows[:, off + col], 0.0)
        halves.append(part.reshape(rows.shape[0], -1))
    table = jnp.concatenate(halves, 1)
    return jnp.pad(table, ((0, 8 - table.shape[0]), (0, 0)))


def _ffn(xb, w_up, w_down, convp, layer, seq_len, tm=512, tf=512):
    m, d = xb.shape
    dff = w_down.shape[1]
    w_up2 = w_up.reshape(-1, w_up.shape[-1])
    w_down2 = w_down.reshape(-1, d)
    starts = _ffn_block_starts(dff, tf)
    nf = len(starts)
    hr = BF16_SUBLANES
    assert dff % LANES == 0 and tf % LANES == 0

    def start(j, base=0):
        return (base // LANES + jnp.minimum(j * (tf // LANES), (dff - tf) // LANES)) * LANES

    kern = functools.partial(_ffn_kernel, tm=tm, blocks_per_seq=seq_len // tm)
    return pl.pallas_call(
        kern,
        grid=(m // tm, nf),
        in_specs=[pl.BlockSpec((tm, d), lambda i, j: (i, 0)),
                  pl.BlockSpec((hr, d), lambda i, j: (jnp.maximum(i * (tm // hr) - 1, 0), 0)),
                  pl.BlockSpec((pl.Element(d), pl.Element(tf)), lambda i, j: (layer * d, start(j))),
                  pl.BlockSpec((pl.Element(d), pl.Element(tf)),
                               lambda i, j: (layer * d, start(j, dff))),
                  pl.BlockSpec((8, tf), lambda i, j: (0, j)),
                  pl.BlockSpec((8, tf), lambda i, j: (0, nf + j)),
                  pl.BlockSpec((pl.Element(tf), pl.Element(d)),
                               lambda i, j: (start(j, layer * dff), 0))],
        out_specs=pl.BlockSpec((tm, d), lambda i, j: (i, 0)),
        out_shape=jax.ShapeDtypeStruct((m, d), F32),
        scratch_shapes=[pltpu.VMEM((tm + hr, d), BF16),
                        pltpu.VMEM((tm + hr, tf), F32),
                        pltpu.VMEM((tm + hr, tf), F32)],
        compiler_params=_params("parallel", "arbitrary"),
        name="conv_ffn",
    )(xb, xb, w_up2, w_up2, convp, convp, w_down2)


def _ple_ln_kernel(xb_ref, x_ref, f_ref, p_ref, wg_ref, wp_ref, g_ref, b_ref, of_ref, ob_ref):
    gate = jax.nn.sigmoid(_dot(xb_ref[...], wg_ref[...]))
    proj = _dot(p_ref[...].astype(BF16), wp_ref[...])
    y = _layer_norm(ALPHA * x_ref[...] + f_ref[...] + gate * proj, g_ref[...], b_ref[...])
    of_ref[...] = y
    ob_ref[...] = y.astype(BF16)


def _ple_ln(xb, x, f, p, wg, wp, g, b, tm=256):
    m, d = x.shape
    pd = p.shape[1]
    row = lambda i: (i, 0)
    const = lambda i: (0, 0)
    return pl.pallas_call(
        _ple_ln_kernel,
        grid=(m // tm,),
        in_specs=[pl.BlockSpec((tm, d), row), pl.BlockSpec((tm, d), row),
                  pl.BlockSpec((tm, d), row), pl.BlockSpec((tm, pd), row),
                  pl.BlockSpec((d, d), const), pl.BlockSpec((pd, d), const),
                  pl.BlockSpec((1, d), const), pl.BlockSpec((1, d), const)],
        out_specs=[pl.BlockSpec((tm, d), row), pl.BlockSpec((tm, d), row)],
        out_shape=[jax.ShapeDtypeStruct((m, d), F32), jax.ShapeDtypeStruct((m, d), BF16)],
        compiler_params=_params("parallel"),
        name="ple_ln",
    )(xb, x, f, p, wg, wp, g, b)


def _split3(a):
    hi = a.astype(BF16)
    r1 = a - hi.astype(F32)
    mid = r1.astype(BF16)
    lo = (r1 - mid.astype(F32)).astype(BF16)
    return hi, mid, lo


def _gla_kernel(q_ref, k_ref, v_ref, r_ref, t_ref, wa2_ref, ba_ref, ng_ref, o_ref, s_ref, *, chunk):
    c = pl.program_id(2)
    dk = q_ref.shape[1]

    @pl.when(c == 0)
    def _():
        s_ref[...] = jnp.zeros_like(s_ref)

    z = _dot(t_ref[...].astype(BF16), wa2_ref[...]) + ba_ref[...]
    la = (jnp.minimum(z, 0.0) - jnp.log1p(jnp.exp(-jnp.abs(z)))) * (1.0 / GLA_TAU)
    row = lax.broadcasted_iota(jnp.int32, (chunk, chunk), 0)
    col = lax.broadcasted_iota(jnp.int32, (chunk, chunk), 1)
    causal = row >= col
    tri = jnp.where(causal, 1.0, 0.0).astype(BF16)
    hi, mid, lo = _split3(la)
    b = _dot(tri, hi) + _dot(tri, mid) + _dot(tri, lo)

    eb = jnp.exp(b)
    qs = q_ref[...].astype(F32) * (dk ** -0.5)
    kf = k_ref[...].astype(F32)
    v = v_ref[...]
    state = s_ref[...]

    o = _dot((qs * eb).astype(BF16), state.astype(BF16))

    sub = chunk // GLA_SUBBLOCKS
    rows = []
    for blk in range(GLA_SUBBLOCKS):
        lo, hi = blk * sub, (blk + 1) * sub
        ref = b[lo - 1:lo, :] if blk else jnp.zeros((1, dk), F32)
        qb = (qs[lo:hi] * jnp.exp(b[lo:hi] - ref)).astype(BF16)
        kb = (kf[:hi] * jnp.exp(jnp.minimum(ref - b[:hi], GLA_EXP_CLAMP))).astype(BF16)
        part = _dot_nt(qb, kb)
        if hi < chunk:
            part = jnp.concatenate([part, jnp.zeros((sub, chunk - hi), F32)], axis=1)
        rows.append(part)
    scores = jnp.where(causal, jnp.concatenate(rows, axis=0), 0.0).astype(BF16)
    o = o + _dot(scores, v)

    eb_last = eb[chunk - LANES:, :].T[:, LANES - 1:LANES]
    kd = kf * jnp.exp(b[chunk - 1:chunk, :] - b)
    s_ref[...] = state * eb_last + _dot(kd.T.astype(BF16), v)

    o = o * lax.rsqrt(jnp.mean(o * o, -1, keepdims=True) + LN_EPS) * ng_ref[...]
    r = r_ref[...].astype(F32)
    o_ref[...] = (o * (r * jax.nn.sigmoid(r))).astype(BF16)


def _gla(qkvr, t, wa2, ba, ng, batch, seq_len, chunk=256):
    m = qkvr.shape[0]
    h = GLA_HEADS
    dk = wa2.shape[1] // h
    dv = ng.shape[1] // h
    nc = seq_len // chunk
    rows = lambda b, hh, c: b * nc + c
    kern = functools.partial(_gla_kernel, chunk=chunk)
    return pl.pallas_call(
        kern,
        grid=(batch, h, nc),
        in_specs=[pl.BlockSpec((chunk, dk), lambda b, hh, c: (rows(b, hh, c), hh)),
                  pl.BlockSpec((chunk, dk), lambda b, hh, c: (rows(b, hh, c), h + hh)),
                  pl.BlockSpec((chunk, dv), lambda b, hh, c: (rows(b, hh, c), (2 * h * dk) // dv + hh)),
                  pl.BlockSpec((chunk, dv), lambda b, hh, c: (rows(b, hh, c), (2 * h * dk) // dv + h + hh)),
                  pl.BlockSpec((chunk, LANES), lambda b, hh, c: (rows(b, hh, c), 0)),
                  pl.BlockSpec((LANES, dk), lambda b, hh, c: (0, hh)),
                  pl.BlockSpec((1, dk), lambda b, hh, c: (0, hh)),
                  pl.BlockSpec((1, dv), lambda b, hh, c: (0, hh))],
        out_specs=pl.BlockSpec((chunk, dv), lambda b, hh, c: (rows(b, hh, c), hh)),
        out_shape=jax.ShapeDtypeStruct((m, h * dv), BF16),
        scratch_shapes=[pltpu.VMEM((dk, dv), F32)],
        compiler_params=_params("parallel", "parallel", "arbitrary"),
        name="gla_chunk",
    )(qkvr, qkvr, qkvr, qkvr, t, wa2, ba, ng)


def _pool_kernel(x_ref, halo_ref, w_ref, sc_ref, g_ref, b_ref, of_ref, ob_ref, xe_ref,
                 *, tm, blocks_per_seq, halo_rows):
    i = pl.program_id(0)
    blk = i % blocks_per_seq
    halo = halo_ref[...]
    xe_ref[0:halo_rows, :] = jnp.where(blk == 0, jnp.zeros_like(halo), halo)
    xe_ref[halo_rows:, :] = x_ref[...]
    gw = w_ref.shape[1]
    pos = blk * tm + lax.broadcasted_iota(jnp.int32, (tm, 1), 0)
    ys = []
    for g, w in enumerate(POOL_WINDOWS):
        cols = slice(g * gw, (g + 1) * gw)
        xg = xe_ref[pl.ds(halo_rows, tm), cols]
        acc = xg
        for dlt in range(1, w):
            acc = acc + xe_ref[pl.ds(halo_rows - dlt, tm), cols]
        cnt = jnp.minimum(pos + 1, w).astype(F32)
        pooled = acc / cnt - xg
        ys.append(_dot(pooled.astype(BF16), w_ref[g]))
    y = jnp.concatenate(ys, axis=-1) * sc_ref[...]
    out = _layer_norm(ALPHA * x_ref[...] + y, g_ref[...], b_ref[...])
    of_ref[...] = out
    ob_ref[...] = out.astype(BF16)


def _pool(x, w, scale, g, b, seq_len, tm=256):
    m, d = x.shape
    halo_rows = max(POOL_WINDOWS)
    kern = functools.partial(_pool_kernel, tm=tm, blocks_per_seq=seq_len // tm, halo_rows=halo_rows)
    row = lambda i: (i, 0)
    const = lambda i: (0, 0)
    return pl.pallas_call(
        kern,
        grid=(m // tm,),
        in_specs=[pl.BlockSpec((tm, d), row),
                  pl.BlockSpec((halo_rows, d), lambda i: (jnp.maximum(i * (tm // halo_rows) - 1, 0), 0)),
                  pl.BlockSpec(w.shape, lambda i: (0, 0, 0)),
                  pl.BlockSpec((1, d), const), pl.BlockSpec((1, d), const), pl.BlockSpec((1, d), const)],
        out_specs=[pl.BlockSpec((tm, d), row), pl.BlockSpec((tm, d), row)],
        out_shape=[jax.ShapeDtypeStruct((m, d), F32), jax.ShapeDtypeStruct((m, d), BF16)],
        scratch_shapes=[pltpu.VMEM((tm + halo_rows, d), F32)],
        compiler_params=_params("parallel"),
        name="pool_mixer",
    )(x, x, w, scale, g, b)


def _ki_prep_kernel(kw_ref, g_ref, b_ref, o_ref, wt_ref):
    x = kw_ref[...]
    wt_ref[...] = x.T * ((DSA_IDX_HEADS ** -0.5) * (DSA_IDX_DIM ** -0.5))
    lane = lax.broadcasted_iota(jnp.int32, x.shape, 1)
    isk = lane < DSA_IDX_DIM
    mu = jnp.sum(jnp.where(isk, x, 0.0), -1, keepdims=True) * (1.0 / DSA_IDX_DIM)
    xc = jnp.where(isk, x - mu, 0.0)
    var = jnp.sum(xc * xc, -1, keepdims=True) * (1.0 / DSA_IDX_DIM)
    kn = jnp.where(isk, xc * lax.rsqrt(var + LN_EPS) * g_ref[...] + b_ref[...], 0.0)
    o_ref[:, 0:LANES] = kn.astype(BF16)
    o_ref[:, LANES:2 * LANES] = pltpu.roll(kn, DSA_IDX_DIM, axis=1).astype(BF16)


def _ki_prep(kiwi, g, b, tm=1024):
    m = kiwi.shape[0]
    return pl.pallas_call(
        _ki_prep_kernel,
        grid=(m // tm,),
        in_specs=[pl.BlockSpec((tm, LANES), lambda i: (i, 0)),
                  pl.BlockSpec((1, LANES), lambda i: (0, 0)),
                  pl.BlockSpec((1, LANES), lambda i: (0, 0))],
        out_specs=[pl.BlockSpec((tm, 2 * LANES), lambda i: (i, 0)),
                   pl.BlockSpec((LANES, tm), lambda i: (0, i))],
        out_shape=[jax.ShapeDtypeStruct((m, 2 * LANES), BF16),
                   jax.ShapeDtypeStruct((LANES, m), F32)],
        compiler_params=_params("parallel"),
        name="dsa_ki_prep",
    )(kiwi, g, b)


def _key_to_float(key):
    bits = jnp.where(key < 0, key ^ 0x7FFFFFFF, key)
    return lax.bitcast_convert_type(bits, F32)


def _dsa_index_kernel(qi_ref, wt_ref, kk_ref, o_ref, sc_ref, *, tq, tk, topk):
    i = pl.program_id(1)
    nk = sc_ref.shape[0]
    t0 = i * tq
    n_chunks = (t0 + tq + tk - 1) // tk
    spos = lax.broadcasted_iota(jnp.int32, (tk, tq), 0)
    tpos = t0 + lax.broadcasted_iota(jnp.int32, (tk, tq), 1)

    def score_chunk(c, carry):
        start = pl.multiple_of(c * tk, tk)
        ka = kk_ref[pl.ds(start, tk), 0:LANES]
        kb = kk_ref[pl.ds(start, tk), LANES:2 * LANES]
        acc = jnp.zeros((tk, tq), F32)
        for pair in range(DSA_IDX_HEADS // 2):
            qp = qi_ref[:, pair * LANES:(pair + 1) * LANES]
            row = DSA_IDX_DIM + 2 * pair
            acc = acc + jnp.maximum(_dot_nt(ka, qp), 0.0) * wt_ref[row:row + 1, :]
            acc = acc + jnp.maximum(_dot_nt(kb, qp), 0.0) * wt_ref[row + 1:row + 2, :]
        sc_ref[c] = jnp.where(c * tk + spos <= tpos, acc, -jnp.inf)
        return carry

    lax.fori_loop(0, n_chunks, score_chunk, 0)

    acc_rows = 32

    def count(thr, strict):
        def body(c, cnt):
            s = sc_ref[c]
            hit = jnp.where(s > thr if strict else s >= thr, 1.0, 0.0)
            return cnt + jnp.sum(hit.reshape(tk // acc_rows, acc_rows, tq), axis=0)
        cnt = lax.fori_loop(0, n_chunks, body, jnp.zeros((acc_rows, tq), F32))
        return jnp.sum(cnt, 0, keepdims=True)

    def bisect(step, key):
        cand = key + jnp.left_shift(jnp.int32(1), 31 - step)
        ok = count(_key_to_float(cand), False) >= float(topk)
        return jnp.where(ok, cand, key)

    key = lax.fori_loop(0, 32, bisect, jnp.full((1, tq), INT_MIN, jnp.int32))
    thr = _key_to_float(jnp.maximum(key, KEY_NEG_FLT_MAX))
    tied = jnp.max(count(thr, False)) > float(topk)

    @pl.when(jnp.logical_not(tied))
    def _():
        def write_chunk(c, carry):
            o_ref[0, c] = jnp.where(sc_ref[c] >= thr, 0.0, NEG_BIG).astype(BF16)
            return carry
        lax.fori_loop(0, n_chunks, write_chunk, 0)

    @pl.when(tied)
    def _():
        need = float(topk) - count(thr, True)
        before = (lax.broadcasted_iota(jnp.int32, (tk, tk), 1)
                  < lax.broadcasted_iota(jnp.int32, (tk, tk), 0))
        before = jnp.where(before, 1.0, 0.0).astype(BF16)

        def write_chunk(c, seen):
            s = sc_ref[c]
            eq = jnp.where(s == thr, 1.0, 0.0)
            rank = _dot(before, eq.astype(BF16)) + seen
            keep = jnp.where(s > thr, 1.0, eq * jnp.where(rank < need, 1.0, 0.0))
            o_ref[0, c] = jnp.where(keep > 0.0, 0.0, NEG_BIG).astype(BF16)
            return seen + jnp.sum(eq, 0, keepdims=True)
        lax.fori_loop(0, n_chunks, write_chunk, jnp.zeros((1, tq), F32))

    def fill_chunk(c, carry):
        o_ref[0, c] = jnp.full((tk, tq), NEG_BIG, BF16)
        return carry

    lax.fori_loop(n_chunks, nk, fill_chunk, 0)


def _dsa_index(proj, wt, kk, batch, seq_len, qi_col_block, tq, tk):
    nq = seq_len // tq
    nk = seq_len // tk
    topk = min(DSA_TOPK_MAX, seq_len // 4)
    qi_w = DSA_IDX_HEADS * DSA_IDX_DIM
    kern = functools.partial(_dsa_index_kernel, tq=tq, tk=tk, topk=topk)
    return pl.pallas_call(
        kern,
        grid=(batch, nq),
        in_specs=[pl.BlockSpec((tq, qi_w), lambda b, i: (b * nq + i, qi_col_block)),
                  pl.BlockSpec((LANES, tq), lambda b, i: (0, b * nq + i)),
                  pl.BlockSpec((seq_len, 2 * LANES), lambda b, i: (b, 0))],
        out_specs=pl.BlockSpec((1, nk, tk, tq), lambda b, i: (b * nq + i, 0, 0, 0)),
        out_shape=jax.ShapeDtypeStruct((batch * nq, nk, tk, tq), BF16),
        scratch_shapes=[pltpu.VMEM((nk, tk, tq), F32)],
        compiler_params=_params("parallel", "parallel"),
        name="dsa_index_topk",
    )(proj, wt, kk)


def _dsa_attn_kernel(qblk_ref, kblk_ref, q_ref, k_ref, v_ref, bias_ref, o_ref,
                     qs_ref, m_ref, l_ref, acc_ref, s_ref, *, tq, tk):
    step = pl.program_id(1)
    i = qblk_ref[step]
    j = kblk_ref[step]
    rep = DSA_HEADS // DSA_KV_HEADS
    dh = k_ref.shape[1] // DSA_KV_HEADS
    last = ((i + 1) * tq - 1) // tk

    @pl.when(j == 0)
    def _():
        m_ref[...] = jnp.full_like(m_ref, NEG_BIG)
        l_ref[...] = jnp.zeros_like(l_ref)
        acc_ref[...] = jnp.zeros_like(acc_ref)
        for hd in range(DSA_HEADS):
            g, r = hd // rep, hd % rep
            qh = q_ref[:, hd * dh:(hd + 1) * dh].astype(F32) * (dh ** -0.5 * LOG2_E)
            qs_ref[g, r * tq:(r + 1) * tq, :] = qh.astype(BF16)

    bias = bias_ref[0, 0]

    def logits(g):
        kg = k_ref[:, g * dh:(g + 1) * dh]
        tops = []
        for r in range(rep):
            cols = slice(r * tq, (r + 1) * tq)
            s = _dot_nt(kg, qs_ref[g, cols, :]).astype(BF16) + bias
            s_ref[g, :, cols] = s
            tops.append(jnp.max(s, 0, keepdims=True))
        return jnp.concatenate(tops, axis=1).astype(F32)

    def accumulate(g, smax):
        vg = v_ref[:, g * dh:(g + 1) * dh]
        m_prev = m_ref[g]
        m_new = jnp.maximum(m_prev, smax)
        alpha = jnp.exp2(m_prev - m_new)
        m_b = m_new.astype(BF16)
        sums, pvs = [], []
        for r in range(rep):
            cols = slice(r * tq, (r + 1) * tq)
            p = jnp.exp2(s_ref[g, :, cols] - m_b[:, cols])
            sums.append(jnp.sum(p.astype(F32), 0, keepdims=True))
            pvs.append(lax.dot_general(vg, p, (((0,), (0,)), ((), ())),
                                       preferred_element_type=F32))
        l_ref[g] = alpha * l_ref[g] + jnp.concatenate(sums, axis=1)
        acc_ref[g] = alpha * acc_ref[g] + jnp.concatenate(pvs, axis=1)
        m_ref[g] = m_new

    smax = logits(0)
    for g in range(DSA_KV_HEADS):
        nxt = logits(g + 1) if g + 1 < DSA_KV_HEADS else None
        accumulate(g, smax)
        smax = nxt

    @pl.when(j == last)
    def _():
        for g in range(DSA_KV_HEADS):
            og = acc_ref[g] / l_ref[g]
            for r in range(rep):
                hd = g * rep + r
                o_ref[:, hd * dh:(hd + 1) * dh] = og[:, r * tq:(r + 1) * tq].T.astype(BF16)


def _dsa_attn(proj, bias, batch, seq_len, dh, tq, tk):
    m = proj.shape[0]
    nq = seq_len // tq
    nk = seq_len // tk
    dq = DSA_HEADS * dh
    dkv = DSA_KV_HEADS * dh
    rep = DSA_HEADS // DSA_KV_HEADS
    pairs = [(i, j) for i in range(nq) for j in range(((i + 1) * tq - 1) // tk + 1)]
    qblk = jnp.asarray([pr[0] for pr in pairs], jnp.int32)
    kblk = jnp.asarray([pr[1] for pr in pairs], jnp.int32)
    kern = functools.partial(_dsa_attn_kernel, tq=tq, tk=tk)
    grid_spec = pltpu.PrefetchScalarGridSpec(
        num_scalar_prefetch=2,
        grid=(batch, len(pairs)),
        in_specs=[pl.BlockSpec((tq, dq), lambda b, s, qb, kb: (b * nq + qb[s], 0)),
                  pl.BlockSpec((tk, dkv), lambda b, s, qb, kb: (b * nk + kb[s], dq // dkv)),
                  pl.BlockSpec((tk, dkv), lambda b, s, qb, kb: (b * nk + kb[s], dq // dkv + 1)),
                  pl.BlockSpec((1, 1, tk, tq), lambda b, s, qb, kb: (b * nq + qb[s], kb[s], 0, 0))],
        out_specs=pl.BlockSpec((tq, dq), lambda b, s, qb, kb: (b * nq + qb[s], 0)),
        scratch_shapes=[pltpu.VMEM((DSA_KV_HEADS, rep * tq, dh), BF16),
                        pltpu.VMEM((DSA_KV_HEADS, 1, rep * tq), F32),
                        pltpu.VMEM((DSA_KV_HEADS, 1, rep * tq), F32),
                        pltpu.VMEM((DSA_KV_HEADS, dh, rep * tq), F32),
                        pltpu.VMEM((DSA_KV_HEADS, tk, rep * tq), BF16)])
    return pl.pallas_call(
        kern,
        grid_spec=grid_spec,
        out_shape=jax.ShapeDtypeStruct((m, dq), BF16),
        compiler_params=_params("parallel", "arbitrary"),
        name="dsa_attention",
    )(qblk, kblk, proj, proj, proj, bias)


def _pad_cols(w, n):
    return jnp.pad(w, ((0, 0), (0, n - w.shape[1])))


def _gla_mixer(xb, batch, seq_len, w_in, w_a1, w_a2, b_a, norm_g):
    qkvr = _matmul(xb, w_in.astype(BF16), BF16, tm=1024, tn=512)
    t = _matmul(xb, _pad_cols(w_a1, LANES).astype(BF16), F32, tm=1024, tn=LANES)
    wa2 = jnp.pad(w_a2, ((0, LANES - w_a2.shape[0]), (0, 0))).astype(BF16)
    return _gla(qkvr, t, wa2, b_a[None, :], norm_g[None, :], batch, seq_len)


def _dsa_mixer(xb, batch, seq_len, w_in, kidx_g, kidx_b):
    d = w_in.shape[0]
    dh = d // DSA_HEADS
    main = d + 2 * DSA_KV_HEADS * dh + DSA_IDX_HEADS * DSA_IDX_DIM
    proj = _matmul(xb, w_in[:, :main].astype(BF16), BF16, tm=1024, tn=512)
    kiwi = _matmul(xb, _pad_cols(w_in[:, main:], LANES).astype(BF16), F32, tm=1024, tn=LANES)
    pad1 = lambda a: jnp.pad(a, (0, LANES - a.shape[0]))[None, :]
    kk, wt = _ki_prep(kiwi, pad1(kidx_g), pad1(kidx_b))
    qi_w = DSA_IDX_HEADS * DSA_IDX_DIM
    tq, tk = 256, 512
    bias = _dsa_index(proj, wt, kk, batch, seq_len, (main - qi_w) // qi_w, tq, tk)
    return _dsa_attn(proj, bias, batch, seq_len, dh, tq, tk)


def kernel(x, p, gla_w_in, gla_w_a1, gla_w_a2, gla_b_a, gla_norm_g, gla_w_o, pool_w, pool_scale,
           dsa_w_in, dsa_kidx_g, dsa_kidx_b, dsa_w_o, ln_mix_g, ln_mix_b, ffn_w_up, ffn_conv_w,
           ffn_conv_b, ffn_w_down, ple_gate_w, ple_proj_w, ln_ffn_g, ln_ffn_b):
    batch, seq_len, d = x.shape
    m = batch * seq_len
    dff = ffn_w_down.shape[1]
    tf = 512
    xf = x.reshape(m, d)
    xb = xf.astype(BF16)
    w_up_b, w_down_b = ffn_w_up.astype(BF16), ffn_w_down.astype(BF16)
    gate_b, proj_b = ple_gate_w.astype(BF16), ple_proj_w.astype(BF16)
    for i in range(DEPTH):
        kind, j = i % N_MIXERS, i // N_MIXERS
        g_mix, b_mix = ln_mix_g[i][None, :], ln_mix_b[i][None, :]
        if kind == 0:
            o = _gla_mixer(xb, batch, seq_len, gla_w_in[j], gla_w_a1[j], gla_w_a2[j], gla_b_a[j],
                           gla_norm_g[j])
            xf, xb = _proj_res_ln(o, gla_w_o[j].astype(BF16), xf, g_mix, b_mix)
        elif kind == 1:
            xf, xb = _pool(xf, pool_w[j].astype(BF16), pool_scale[j][None, :], g_mix, b_mix, seq_len)
        else:
            o = _dsa_mixer(xb, batch, seq_len, dsa_w_in[j], dsa_kidx_g[j], dsa_kidx_b[j])
            xf, xb = _proj_res_ln(o, dsa_w_o[j].astype(BF16), xf, g_mix, b_mix)

        convp = _ffn_conv_params(ffn_conv_w[i], ffn_conv_b[i], dff, tf)
        f = _ffn(xb, w_up_b, w_down_b, convp, i, seq_len, tf=tf)
        xf, xb = _ple_ln(xb, xf, f, p[i].reshape(m, -1), gate_b[i], proj_b[i],
                         ln_ffn_g[i][None, :], ln_ffn_b[i][None, :])
    return xf.reshape(batch, seq_len, d)
```
